```python
import jax, jax.numpy as jnp
from jax import lax
import numpy as np

D_MODEL = 2048
BATCH = 1
SEQ = 8192
DEPTH = 2
DEC_BATCH = 128
DEC_SEQ = 4
PAST_LEN = 2048
PAGE_SIZE = 128

D_CONV = 1024
CONV_W = 31
N_HEADS = 16
N_KV = 4
HEAD_DIM = 64
GRP = N_HEADS // N_KV
L_CMP = 32
D_CMP = 16
CMP_HID = 256
BLK_SLC = 64
TOP_N = 16
WINDOW = 512
Q_BLK = 128
H_HG = 8
DK_HG = 128
DV_HG = 128
HG_CHUNK = 64
N_GROUPS = 4
EXP_PER_GROUP = 8
N_EXPERTS = N_GROUPS * EXP_PER_GROUP
D_EXPERT = 256
TOP_K_IN_GROUP = 2

ALPHA = (2 * DEPTH) ** 0.25
BETA = (8 * DEPTH) ** -0.25
LN_EPS = 1e-5
NEG = -1e30
BIG = 1e9

SPLITS = [2 * D_CONV,
          N_HEADS * HEAD_DIM,
          N_KV * 6 * HEAD_DIM,
          3 * N_HEADS,
          H_HG * DK_HG,
          H_HG * DK_HG,
          H_HG * DV_HG,
          H_HG * DV_HG,
          3 * D_MODEL]
D_IN = sum(SPLITS)

kernel_name = 'hybrid_conv_nsa_hgrn2_hmoe_step'


def layer_norm(x, g, b):
    xf = x.astype(jnp.float32)
    mu = jnp.mean(xf, -1, keepdims=True)
    var = jnp.mean(jnp.square(xf - mu), -1, keepdims=True)
    return ((xf - mu) * lax.rsqrt(var + LN_EPS) * g.astype(jnp.float32) + b.astype(jnp.float32)).astype(x.dtype)


def rms_norm(x, g):
    return x * lax.rsqrt(jnp.mean(jnp.square(x), -1, keepdims=True) + LN_EPS) * g.astype(jnp.float32)


def ada(c, w, b):
    m = (jax.nn.silu(c) @ w + b)[:, None, :]
    shift, scale, gate = jnp.split(m, 3, axis=-1)
    return shift, 1.0 + scale, 1.0 + gate


def conv_branch(z, prev, w_dw, b_dw, ln_g, ln_b, w_out):
    a, g = jnp.split(z, 2, axis=-1)
    u = a * jax.nn.sigmoid(g)
    padded = jnp.concatenate([prev.astype(u.dtype), u], axis=1)
    y = lax.conv_general_dilated(padded, w_dw[:, None, :].astype(u.dtype), window_strides=(1,),
                                 padding='VALID', dimension_numbers=('NWC', 'WIO', 'NWC'),
                                 feature_group_count=D_CONV) + b_dw
    y = jax.nn.silu(layer_norm(y, ln_g, ln_b))
    return y @ w_out, padded[:, -(CONV_W - 1):]


def gated_recurrence(q, k, v, logf, s0):
    B, T, H, DK = q.shape
    C = min(HG_CHUNK, T)
    n = T // C

    def blk(a):
        return jnp.moveaxis(a.reshape(B, n, C, *a.shape[2:]), 1, 0)

    tri = jnp.tril(jnp.ones((C, C), bool))[None, :, :, None, None]

    def step(S, xs):
        qc, kc, vc, gc = xs
        G = jnp.cumsum(gc, axis=1)
        o_inter = jnp.einsum('bchk,bhkv->bchv', qc * jnp.exp(G), S)
        decay = jnp.exp(jnp.where(tri, G[:, :, None] - G[:, None, :], -jnp.inf))
        A = jnp.einsum('bthk,bshk,btshk->bths', qc, kc, decay)
        o_intra = jnp.einsum('bths,bshv->bthv', A, vc)
        G_last = G[:, -1]
        S_new = jnp.exp(G_last)[..., None] * S + jnp.einsum('bshk,bshv->bhkv', kc * jnp.exp(G_last[:, None] - G), vc)
        return S_new, o_inter + o_intra

    s_fin, o = lax.scan(step, s0, (blk(q), blk(k), blk(v), blk(logf)))
    return jnp.moveaxis(o, 0, 1).reshape(B, T, H, v.shape[-1]), s_fin


def hgrn_branch(zq, zf, zi, zg, s0, lb, norm_g, w_proj):
    B, T = zq.shape[:2]
    f32 = jnp.float32
    q = jax.nn.silu(zq.astype(f32)).reshape(B, T, H_HG, DK_HG)
    lbh = lb.reshape(H_HG, DK_HG)
    f = lbh + (1.0 - lbh) * jax.nn.sigmoid(zf.astype(f32)).reshape(B, T, H_HG, DK_HG)
    v = zi.astype(f32).reshape(B, T, H_HG, DV_HG)
    o, s_new = gated_recurrence(q, 1.0 - f, v, jnp.log(f), s0.astype(f32))
    o = rms_norm(o, norm_g.reshape(H_HG, DV_HG)) * jax.nn.silu(zg.astype(f32)).reshape(B, T, H_HG, DV_HG)
    return o.reshape(B, T, H_HG * DV_HG).astype(zq.dtype) @ w_proj, s_new.astype(s0.dtype)


def nsa_compress(kv, pe, w1, b1, w2):
    B, L = kv.shape[:2]
    n16 = L // D_CMP
    x = kv[:, :n16 * D_CMP].reshape(B, n16, D_CMP, N_KV, 2, HEAD_DIM)
    pe_t = jnp.transpose(pe, (1, 0, 2))
    ha = jnp.einsum('bnjgsd,sjdh->bngsh', x + pe_t[None, None, :D_CMP, None], w1[:, :D_CMP])
    hb = jnp.einsum('bnjgsd,sjdh->bngsh', x + pe_t[None, None, D_CMP:, None], w1[:, D_CMP:])
    h = jax.nn.silu(ha[:, :-1] + hb[:, 1:] + b1)
    return jnp.einsum('bngsh,shd->bngsd', h, w2)


def nsa_attend(q, gates, qpos, cmp_kv, slc_kv, win_kv, wpos):
    f32 = jnp.float32
    scale = HEAD_DIM ** -0.5
    B, Tq = q.shape[:2]
    L = slc_kv.shape[1]
    n_cmp = cmp_kv.shape[1]
    cmp_last = jnp.arange(n_cmp) * D_CMP + L_CMP - 1
    vis = (cmp_last[None, :] <= qpos[:, None])[None, :, None, None, :]
    s = jnp.einsum('btgrd,bigd->btgri', q, cmp_kv[..., 0, :]).astype(f32) * scale
    p_cmp = jax.nn.softmax(jnp.where(vis, s, NEG), axis=-1) * vis
    o_cmp = jnp.einsum('btgri,bigd->btgrd', p_cmp.astype(q.dtype), cmp_kv[..., 1, :])
    n_slc = -(-L // BLK_SLC)
    i_start = jnp.arange(n_cmp) * D_CMP
    j_start = jnp.arange(n_slc) * BLK_SLC
    overlap = ((i_start[:, None] < j_start[None] + BLK_SLC) & (i_start[:, None] + L_CMP > j_start[None])).astype(f32)
    imp = jnp.einsum('btgri,ij->btgj', p_cmp, overlap)
    cur = qpos // BLK_SLC
    jj = jnp.arange(n_slc)[None]
    valid = (jj <= cur[:, None])[None, :, None]
    forced = ((jj == 0) | (jj == cur[:, None]) | (jj == cur[:, None] - 1))[None, :, None]
    score = jnp.where(forced, BIG, jnp.where(valid, imp, -BIG))
    k_sel = min(TOP_N, n_slc)
    _, idx = lax.top_k(score, k_sel)
    pos = (idx[..., None] * BLK_SLC + jnp.arange(BLK_SLC)).reshape(B, Tq, N_KV, k_sel * BLK_SLC)
    ok = (pos <= qpos[None, :, None, None])[:, :, :, None, :]
    bi = jnp.arange(B)[:, None, None, None]
    gi = jnp.arange(N_KV)[None, None, :, None]
    kv_sel = slc_kv[bi, jnp.minimum(pos, L - 1), gi]
    s = jnp.einsum('btgrd,btgkd->btgrk', q, kv_sel[..., 0, :]).astype(f32) * scale
    p = jax.nn.softmax(jnp.where(ok, s, NEG), axis=-1)
    o_slc = jnp.einsum('btgrk,btgkd->btgrd', p.astype(q.dtype), kv_sel[..., 1, :])
    wok = ((wpos[None] <= qpos[:, None]) & (wpos[None] > qpos[:, None] - WINDOW) & (wpos[None] >= 0))[None, :, None, None, :]
    s = jnp.einsum('btgrd,bsgd->btgrs', q, win_kv[..., 0, :]).astype(f32) * scale
    p = jax.nn.softmax(jnp.where(wok, s, NEG), axis=-1)
    o_win = jnp.einsum('btgrs,bsgd->btgrd', p.astype(q.dtype), win_kv[..., 1, :])
    return gates[..., 0:1] * o_cmp + gates[..., 1:2] * o_slc + gates[..., 2:3] * o_win


def nsa_prompt(zq, gates, kv6, P, l):
    B, T = zq.shape[:2]
    cmp_kv = nsa_compress(kv6[..., 0:2, :], P['w_cmp_pe'][l], P['w_cmp_1'][l], P['b_cmp_1'][l], P['w_cmp_2'][l])
    slc = kv6[..., 2:4, :]
    win = kv6[..., 4:6, :]
    win_pad = jnp.pad(win, ((0, 0), (WINDOW, 0), (0, 0), (0, 0), (0, 0)))
    nb = T // Q_BLK
    qb = jnp.moveaxis(zq.reshape(B, nb, Q_BLK, N_KV, GRP, HEAD_DIM), 1, 0)
    gb = jnp.moveaxis(gates.reshape(B, nb, Q_BLK, N_KV, GRP, 3), 1, 0)
    starts = jnp.arange(nb, dtype=jnp.int32) * Q_BLK

    def one(args):
        qi, gi, st = args
        qpos = st + jnp.arange(Q_BLK, dtype=jnp.int32)
        wk = lax.dynamic_slice_in_dim(win_pad, st, WINDOW + Q_BLK, axis=1)
        wpos = st - WINDOW + jnp.arange(WINDOW + Q_BLK, dtype=jnp.int32)
        return nsa_attend(qi, gi, qpos, cmp_kv, slc, wk, wpos)

    o = lax.map(one, (qb, gb, starts))
    o = jnp.moveaxis(o, 0, 1).reshape(B, T, N_HEADS * HEAD_DIM)
    return o, kv6[..., 0:4, :], win[:, -min(WINDOW, T):]


def nsa_sample(zq, gates, kv6, pool_l, page_table, win_buf, P, l):
    B, T = zq.shape[:2]
    past = page_table.shape[1] * PAGE_SIZE
    past_cmp = pool_l[page_table, :, :, 0:2].reshape(B, past, N_KV, 2, HEAD_DIM).astype(kv6.dtype)
    past_slc = pool_l[page_table, :, :, 2:4].reshape(B, past, N_KV, 2, HEAD_DIM).astype(kv6.dtype)
    full_cmp = jnp.concatenate([past_cmp, kv6[..., 0:2, :]], axis=1)
    full_slc = jnp.concatenate([past_slc, kv6[..., 2:4, :]], axis=1)
    cmp_kv = nsa_compress(full_cmp, P['w_cmp_pe'][l], P['w_cmp_1'][l], P['b_cmp_1'][l], P['w_cmp_2'][l])
    wlen = win_buf.shape[1]
    wkv = jnp.concatenate([win_buf.astype(kv6.dtype), kv6[..., 4:6, :]], axis=1)
    wpos = past - wlen + jnp.arange(wlen + T, dtype=jnp.int32)
    qpos = past + jnp.arange(T, dtype=jnp.int32)
    o = nsa_attend(zq.reshape(B, T, N_KV, GRP, HEAD_DIM), gates.reshape(B, T, N_KV, GRP, 3), qpos,
                   cmp_kv, full_slc, wkv, wpos)
    return o.reshape(B, T, N_HEADS * HEAD_DIM), kv6[..., 0:4, :], wkv[:, -wlen:]


def token_mixer(u, P, l, lb, conv_prev, s0, nsa_fn):
    B, T = u.shape[:2]
    z = u @ P['w_in'][l]
    z_conv, z_q, z_kv, z_ng, z_hq, z_hf, z_hi, z_hg, z_mg = jnp.split(z, np.cumsum(SPLITS)[:-1].tolist(), axis=-1)
    y_conv, conv_new = conv_branch(z_conv, conv_prev, P['w_dw'][l], P['b_dw'][l], P['conv_ln_g'][l],
                                   P['conv_ln_b'][l], P['w_conv_out'][l])
    kv6 = z_kv.reshape(B, T, N_KV, 6, HEAD_DIM)
    nsa_gates = jax.nn.sigmoid(z_ng.astype(jnp.float32)).astype(u.dtype)
    o_nsa, nsa_rows, win_new = nsa_fn(z_q, nsa_gates, kv6)
    y_nsa = o_nsa @ P['w_proj_nsa'][l]
    y_hg, s_new = hgrn_branch(z_hq, z_hf, z_hi, z_hg, s0, lb, P['hgrn_norm_g'][l], P['w_proj_hgrn'][l])
    g_a, g_b, g_c = jnp.split(jax.nn.sigmoid(z_mg), 3, axis=-1)
    merged = g_a * y_conv + g_b * y_nsa + g_c * y_hg
    return merged @ P['w_out'][l], (nsa_rows, win_new, conv_new, s_new)


def moe(u, P, l):
    f32 = jnp.float32
    B, T = u.shape[:2]
    gl = (u @ P['w_route_group'][l]).astype(f32) + P['b_route_group'][l]
    pg = jax.nn.softmax(gl, axis=-1)
    gsel = jnp.argmax(gl, axis=-1)
    wg = jnp.max(pg, axis=-1)
    el = ((u @ P['w_route_expert'][l]).astype(f32) + P['b_route_expert'][l]).reshape(B, T, N_GROUPS, EXP_PER_GROUP)
    el_g = jnp.sum(el * jax.nn.one_hot(gsel, N_GROUPS, dtype=f32)[..., None], axis=2)
    pe = jax.nn.softmax(el_g, axis=-1)
    top_v, top_i = lax.top_k(pe, TOP_K_IN_GROUP)
    top_v = top_v / jnp.sum(top_v, -1, keepdims=True)
    eid = gsel[..., None] * EXP_PER_GROUP + top_i
    comb = jnp.sum(jax.nn.one_hot(eid, N_EXPERTS, dtype=f32) * (wg[..., None] * top_v)[..., None], axis=-2)
    h = jax.nn.silu(jnp.einsum('btd,edf->btef', u, P['w_exp_gate'][l])) * jnp.einsum('btd,edf->btef', u, P['w_exp_up'][l])
    h = h * comb[..., None].astype(h.dtype)
    return jnp.einsum('btef,efd->btd', h, P['w_exp_down'][l])


def run_layer(x, c, P, l, lb, conv_prev, s0, nsa_fn):
    shift, scale, gate = ada(c, P['w_ada'][l, 0], P['b_ada'][l, 0])
    y, st = token_mixer(x * scale + shift, P, l, lb, conv_prev, s0, nsa_fn)
    x = layer_norm(ALPHA * x + gate * y, P['ln_g'][l, 0], P['ln_b'][l, 0])
    shift, scale, gate = ada(c, P['w_ada'][l, 1], P['b_ada'][l, 1])
    y = moe(x * scale + shift, P, l)
    x = layer_norm(ALPHA * x + gate * y, P['ln_g'][l, 1], P['ln_b'][l, 1])
    return x, st


def setup_inputs(seed: int = 0) -> dict:
    key = jax.random.key(seed)
    ks = iter(jax.random.split(key, 48))
    f32 = jnp.float32

    def nrm(shape, scale=1.0):
        return jax.random.normal(next(ks), shape, f32) * scale

    n_pages = PAST_LEN // PAGE_SIZE
    n_used = DEC_BATCH * n_pages
    n_pool = n_used + max(1, n_used // 4)
    w_buf = min(WINDOW, PAST_LEN)
    D = D_MODEL
    inputs = {}
    inputs['x_prompt'] = nrm((BATCH, SEQ, D))
    inputs['x_sample'] = nrm((DEC_BATCH, DEC_SEQ, D))
    inputs['cache_nsa_kv'] = nrm((DEPTH, n_pool, PAGE_SIZE, N_KV, 4, HEAD_DIM))
    inputs['state_win_kv'] = nrm((DEPTH, DEC_BATCH, w_buf, N_KV, 2, HEAD_DIM))
    inputs['state_conv'] = nrm((DEPTH, DEC_BATCH, CONV_W - 1, D_CONV), 0.5)
    inputs['state_hgrn'] = nrm((DEPTH, DEC_BATCH, H_HG, DK_HG, DV_HG), 0.3)
    inputs['page_table'] = jax.random.permutation(next(ks), n_pool)[:n_used].reshape(DEC_BATCH, n_pages).astype(jnp.int32)
    inputs['c_prompt'] = nrm((BATCH, D))
    inputs['c_sample'] = nrm((DEC_BATCH, D))
    inputs['w_ada'] = nrm((DEPTH, 2, D, 3 * D), 0.2 * D ** -0.5)
    inputs['b_ada'] = nrm((DEPTH, 2, 3 * D), 0.02)
    inputs['w_in'] = nrm((DEPTH, D, D_IN), D ** -0.5)
    inputs['w_cmp_pe'] = nrm((DEPTH, 2, L_CMP, HEAD_DIM), 0.5)
    inputs['w_cmp_1'] = nrm((DEPTH, 2, L_CMP, HEAD_DIM, CMP_HID), (L_CMP * HEAD_DIM) ** -0.5)
    inputs['b_cmp_1'] = nrm((DEPTH, 2, CMP_HID), 0.02)
    inputs['w_cmp_2'] = nrm((DEPTH, 2, CMP_HID, HEAD_DIM), 1.5 * CMP_HID ** -0.5)
    inputs['w_dw'] = nrm((DEPTH, CONV_W, D_CONV), CONV_W ** -0.5)
    inputs['b_dw'] = nrm((DEPTH, D_CONV), 0.02)
    inputs['conv_ln_g'] = 1.0 + nrm((DEPTH, D_CONV), 0.02)
    inputs['conv_ln_b'] = nrm((DEPTH, D_CONV), 0.02)
    inputs['w_conv_out'] = nrm((DEPTH, D_CONV, D), D_CONV ** -0.5)
    inputs['hgrn_lb_logits'] = nrm((DEPTH, H_HG * DK_HG), 0.5)
    inputs['hgrn_norm_g'] = 1.0 + nrm((DEPTH, H_HG * DV_HG), 0.02)
    inputs['w_proj_nsa'] = nrm((DEPTH, N_HEADS * HEAD_DIM, D), (N_HEADS * HEAD_DIM) ** -0.5)
    inputs['w_proj_hgrn'] = nrm((DEPTH, H_HG * DV_HG, D), (H_HG * DV_HG) ** -0.5)
    inputs['w_out'] = nrm((DEPTH, D, D), BETA * D ** -0.5)
    inputs['ln_g'] = 1.0 + nrm((DEPTH, 2, D), 0.02)
    inputs['ln_b'] = nrm((DEPTH, 2, D), 0.02)
    inputs['w_route_group'] = nrm((DEPTH, D, N_GROUPS), D ** -0.5)
    inputs['b_route_group'] = nrm((DEPTH, N_GROUPS), 0.01)
    inputs['w_route_expert'] = nrm((DEPTH, D, N_EXPERTS), D ** -0.5)
    inputs['b_route_expert'] = nrm((DEPTH, N_EXPERTS), 0.01)
    inputs['w_exp_gate'] = nrm((DEPTH, N_EXPERTS, D, D_EXPERT), D ** -0.5)
    inputs['w_exp_up'] = nrm((DEPTH, N_EXPERTS, D, D_EXPERT), D ** -0.5)
    inputs['w_exp_down'] = nrm((DEPTH, N_EXPERTS, D_EXPERT, D), BETA * D_EXPERT ** -0.5)
    return inputs


def reference(x_prompt, x_sample, cache_nsa_kv, state_win_kv, state_conv, state_hgrn, page_table,
              c_prompt, c_sample, w_ada, b_ada, w_in, w_cmp_pe, w_cmp_1, b_cmp_1, w_cmp_2,
              w_dw, b_dw, conv_ln_g, conv_ln_b, w_conv_out, hgrn_lb_logits, hgrn_norm_g,
              w_proj_nsa, w_proj_hgrn, w_out, ln_g, ln_b, w_route_group, b_route_group,
              w_route_expert, b_route_expert, w_exp_gate, w_exp_up, w_exp_down):
    P = dict(w_ada=w_ada, b_ada=b_ada, w_in=w_in, w_cmp_pe=w_cmp_pe, w_cmp_1=w_cmp_1, b_cmp_1=b_cmp_1,
             w_cmp_2=w_cmp_2, w_dw=w_dw, b_dw=b_dw, conv_ln_g=conv_ln_g, conv_ln_b=conv_ln_b,
             w_conv_out=w_conv_out, hgrn_norm_g=hgrn_norm_g, w_proj_nsa=w_proj_nsa,
             w_proj_hgrn=w_proj_hgrn, w_out=w_out, ln_g=ln_g, ln_b=ln_b, w_route_group=w_route_group,
             b_route_group=b_route_group, w_route_expert=w_route_expert, b_route_expert=b_route_expert,
             w_exp_gate=w_exp_gate, w_exp_up=w_exp_up, w_exp_down=w_exp_down)
    p_lb = jax.nn.softmax(hgrn_lb_logits.astype(jnp.float32), axis=0)
    lb_all = jnp.cumsum(p_lb, axis=0) - p_lb[0:1]
    xp, xs = x_prompt, x_sample
    bp = x_prompt.shape[0]
    p_kv, p_win, p_conv, p_hg = [], [], [], []
    s_kv, s_win, s_conv, s_hg = [], [], [], []
    for l in range(DEPTH):
        conv0 = jnp.zeros((bp, CONV_W - 1, D_CONV), xp.dtype)
        s0 = jnp.zeros((bp, H_HG, DK_HG, DV_HG), xp.dtype)
        xp, (kv_r, win_r, conv_r, hg_r) = run_layer(
            xp, c_prompt, P, l, lb_all[l], conv0, s0,
            lambda q, g, kv: nsa_prompt(q, g, kv, P, l))
        p_kv.append(kv_r); p_win.append(win_r); p_conv.append(conv_r); p_hg.append(hg_r)
        xs, (kv_r, win_r, conv_r, hg_r) = run_layer(
            xs, c_sample, P, l, lb_all[l], state_conv[l], state_hgrn[l],
            lambda q, g, kv: nsa_sample(q, g, kv, cache_nsa_kv[l], page_table, state_win_kv[l], P, l))
        s_kv.append(kv_r); s_win.append(win_r); s_conv.append(conv_r); s_hg.append(hg_r)
    return (xp, xs,
            jnp.stack(p_kv), jnp.stack(p_win), jnp.stack(p_conv), jnp.stack(p_hg),
            jnp.stack(s_kv), jnp.stack(s_win), jnp.stack(s_conv), jnp.stack(s_hg))
```

```python
import functools

import numpy as np
import jax
import jax.numpy as jnp
from jax import lax
from jax.experimental import pallas as pl
from jax.experimental.pallas import tpu as pltpu

F32 = jnp.float32
BF16 = jnp.bfloat16

D_MODEL = 2048
DEPTH = 2
D_CONV = 1024
CONV_W = 31
N_HEADS = 16
N_KV = 4
HEAD_DIM = 64
GRP = N_HEADS // N_KV
L_CMP = 32
D_CMP = 16
CMP_HID = 256
BLK_SLC = 64
TOP_N = 16
WINDOW = 512
H_HG = 8
DK_HG = 128
DV_HG = 128
HG_CHUNK = 64
N_GROUPS = 4
EXP_PER_GROUP = 8
N_EXPERTS = N_GROUPS * EXP_PER_GROUP
D_EXPERT = 256
PAGE_SIZE = 128

ALPHA = (2 * DEPTH) ** 0.25
LN_EPS = 1e-5
NEG = -1e30
BIG = 1e9

OFF_MG = 0
OFF_CONV = 3 * D_MODEL
OFF_Q = OFF_CONV + 2 * D_CONV
OFF_KV = OFF_Q + N_HEADS * HEAD_DIM
OFF_HG = OFF_KV + N_KV * 6 * HEAD_DIM
OFF_NG = OFF_HG + 4 * H_HG * DK_HG
NZ = OFF_NG + 128
NZ_TILE = 1152
ROUTE_W = 128


def _params(sem, vmem_mb=48):
    return pltpu.CompilerParams(dimension_semantics=sem, vmem_limit_bytes=vmem_mb * 1024 * 1024)


def _sigmoid(x):
    return 1.0 / (1.0 + jnp.exp(-x))


def _silu(x):
    return x * _sigmoid(x)


def _bdot(a, b):
    return jnp.dot(a.astype(BF16), b.astype(BF16), preferred_element_type=F32)


def _row_spec(arr, tm):
    d = arr.shape[1]
    if arr.shape[0] == 1:
        return pl.BlockSpec((1, d), lambda i: (0, 0))
    return pl.BlockSpec((tm, d), lambda i: (i, 0))


def _mm_kernel(x_ref, w_ref, o_ref):
    o_ref[...] = _bdot(x_ref[...], w_ref[...]).astype(o_ref.dtype)


def matmul(x, w, *, tn, out_dtype=F32, tm=1024):
    m, k = x.shape
    n = w.shape[1]
    tm = min(tm, m)
    return pl.pallas_call(
        _mm_kernel,
        grid=(m // tm, n // tn),
        in_specs=[pl.BlockSpec((tm, k), lambda i, j: (i, 0)),
                  pl.BlockSpec((k, tn), lambda i, j: (0, j))],
        out_specs=pl.BlockSpec((tm, tn), lambda i, j: (i, j)),
        out_shape=jax.ShapeDtypeStruct((m, n), out_dtype),
        compiler_params=_params(("parallel", "arbitrary")),
        name="matmul",
    )(x, w)


def _ada_kernel(c_ref, w_ref, b_ref, o_ref, *, tn):
    j = pl.program_id(1)
    m = _bdot(_silu(c_ref[...]), w_ref[0]) + b_ref[0]
    col = j * tn + lax.broadcasted_iota(jnp.int32, m.shape, 1)
    o_ref[0] = m + (col >= D_MODEL).astype(F32)


def ada_all(c, w_ada, b_ada, tn=1024):
    r = c.shape[0]
    n = 3 * D_MODEL
    w = w_ada.reshape(2 * DEPTH, D_MODEL, n)
    b = b_ada.reshape(2 * DEPTH, 1, n)
    return pl.pallas_call(
        functools.partial(_ada_kernel, tn=tn),
        grid=(2 * DEPTH, n // tn),
        in_specs=[pl.BlockSpec((r, D_MODEL), lambda a, j: (0, 0)),
                  pl.BlockSpec((1, D_MODEL, tn), lambda a, j: (a, 0, j)),
                  pl.BlockSpec((1, 1, tn), lambda a, j: (a, 0, j))],
        out_specs=pl.BlockSpec((1, r, tn), lambda a, j: (a, 0, j)),
        out_shape=jax.ShapeDtypeStruct((2 * DEPTH, r, n), F32),
        compiler_params=_params(("parallel", "arbitrary")),
        name="ada",
    )(c, w, b)


def _mod_kernel(x_ref, sc_ref, sh_ref, o_ref):
    o_ref[...] = (x_ref[...] * sc_ref[...] + sh_ref[...]).astype(o_ref.dtype)


def modulate(x, scale, shift, tm=512):
    m, d = x.shape
    tm = min(tm, m)
    return pl.pallas_call(
        _mod_kernel,
        grid=(m // tm,),
        in_specs=[pl.BlockSpec((tm, d), lambda i: (i, 0)), _row_spec(scale, tm), _row_spec(shift, tm)],
        out_specs=pl.BlockSpec((tm, d), lambda i: (i, 0)),
        out_shape=jax.ShapeDtypeStruct((m, d), BF16),
        compiler_params=_params(("parallel",)),
        name="modulate",
    )(x, scale, shift)


def _post_ln(x_ref, y_ref, gate_ref, g_ref, b_ref):
    v = ALPHA * x_ref[...] + gate_ref[...] * y_ref[...]
    mu = jnp.mean(v, axis=-1, keepdims=True)
    d = v - mu
    var = jnp.mean(d * d, axis=-1, keepdims=True)
    return d * lax.rsqrt(var + LN_EPS) * g_ref[...] + b_ref[...]


def _ln_kernel(x_ref, y_ref, gate_ref, g_ref, b_ref, o_ref):
    o_ref[...] = _post_ln(x_ref, y_ref, gate_ref, g_ref, b_ref)


def _split_bf16(a):
    hi = a.astype(BF16)
    lo = (a - hi.astype(F32)).astype(BF16)
    return hi, lo


def _route(logits):
    lane = lax.broadcasted_iota(jnp.int32, logits.shape, 1).astype(F32)
    far = jnp.float32(1e9)
    is_g = (lane >= N_EXPERTS) & (lane < N_EXPERTS + N_GROUPS)
    gl = jnp.where(is_g, logits, -jnp.inf)
    gmax = jnp.max(gl, axis=1, keepdims=True)
    gidx = jnp.min(jnp.where(gl == gmax, lane, far), axis=1, keepdims=True) - N_EXPERTS
    wg = 1.0 / jnp.sum(jnp.where(is_g, jnp.exp(gl - gmax), 0.0), axis=1, keepdims=True)
    lo = gidx * EXP_PER_GROUP
    in_g = (lane >= lo) & (lane < lo + EXP_PER_GROUP)
    el = jnp.where(in_g, logits, -jnp.inf)
    m1 = jnp.max(el, axis=1, keepdims=True)
    i1 = jnp.min(jnp.where(el == m1, lane, far), axis=1, keepdims=True)
    el2 = jnp.where(lane == i1, -jnp.inf, el)
    m2 = jnp.max(el2, axis=1, keepdims=True)
    i2 = jnp.min(jnp.where(el2 == m2, lane, far), axis=1, keepdims=True)
    e2 = jnp.exp(m2 - m1)
    t1 = 1.0 / (1.0 + e2)
    t2 = e2 / (1.0 + e2)
    return jnp.where(lane == i1, wg * t1, 0.0) + jnp.where(lane == i2, wg * t2, 0.0)


def _ln_route_kernel(x_ref, y_ref, gate_ref, g_ref, b_ref, sc_ref, sh_ref, wr_ref, br_ref,
                     xn_ref, u_ref, comb_ref):
    xn = _post_ln(x_ref, y_ref, gate_ref, g_ref, b_ref)
    xn_ref[...] = xn
    u = xn * sc_ref[...] + sh_ref[...]
    u_ref[...] = u.astype(BF16)
    u_hi, u_lo = _split_bf16(u)
    w_hi, w_lo = _split_bf16(wr_ref[...])
    logits = (jnp.dot(u_hi, w_hi, preferred_element_type=F32)
              + jnp.dot(u_lo, w_hi, preferred_element_type=F32)
              + jnp.dot(u_hi, w_lo, preferred_element_type=F32)) + br_ref[...]
    comb_ref[...] = _route(logits)


def ln_residual(x, y, gate, g, b, tm=512):
    m, d = x.shape
    tm = min(tm, m)
    row = pl.BlockSpec((tm, d), lambda i: (i, 0))
    vec = pl.BlockSpec((1, d), lambda i: (0, 0))
    return pl.pallas_call(
        _ln_kernel,
        grid=(m // tm,),
        in_specs=[row, row, _row_spec(gate, tm), vec, vec],
        out_specs=row,
        out_shape=jax.ShapeDtypeStruct((m, d), F32),
        compiler_params=_params(("parallel",)),
        name="ln_residual",
    )(x, y, gate, g, b)


def ln_residual_route(x, y, gate, g, b, scale, shift, w_route, b_route, tm=512):
    m, d = x.shape
    tm = min(tm, m)
    row = pl.BlockSpec((tm, d), lambda i: (i, 0))
    vec = pl.BlockSpec((1, d), lambda i: (0, 0))
    return pl.pallas_call(
        _ln_route_kernel,
        grid=(m // tm,),
        in_specs=[row, row, _row_spec(gate, tm), vec, vec, _row_spec(scale, tm), _row_spec(shift, tm),
                  pl.BlockSpec((d, ROUTE_W), lambda i: (0, 0)), pl.BlockSpec((1, ROUTE_W), lambda i: (0, 0))],
        out_specs=[row, row, pl.BlockSpec((tm, ROUTE_W), lambda i: (i, 0))],
        out_shape=[jax.ShapeDtypeStruct((m, d), F32), jax.ShapeDtypeStruct((m, d), BF16),
                   jax.ShapeDtypeStruct((m, ROUTE_W), F32)],
        compiler_params=_params(("parallel",)),
        name="ln_residual_route",
    )(x, y, gate, g, b, scale, shift, w_route, b_route)


def _merge_kernel(ac_ref, on_ref, oh_ref, wc_ref, wn_ref, wh_ref, za_ref, zb_ref, zc_ref, o_ref):
    yc = _bdot(ac_ref[...], wc_ref[...])
    yn = _bdot(on_ref[...], wn_ref[...])
    yh = _bdot(oh_ref[...], wh_ref[...])
    o = _sigmoid(za_ref[...]) * yc + _sigmoid(zb_ref[...]) * yn + _sigmoid(zc_ref[...]) * yh
    o_ref[...] = o.astype(o_ref.dtype)


def merge_branches(a_conv, o_nsa, o_hg, w_conv_out, w_proj_nsa, w_proj_hgrn, z, tm=512, tn=512):
    m, k = a_conv.shape
    tm = min(tm, m)
    nb = D_MODEL // tn
    act = pl.BlockSpec((tm, k), lambda i, j: (i, 0))
    wsp = pl.BlockSpec((k, tn), lambda i, j: (0, j))
    gate = [pl.BlockSpec((tm, tn), lambda i, j, o=(OFF_MG + c * D_MODEL) // tn: (i, o + j)) for c in range(3)]
    return pl.pallas_call(
        _merge_kernel,
        grid=(m // tm, nb),
        in_specs=[act, act, act, wsp, wsp, wsp] + gate,
        out_specs=pl.BlockSpec((tm, tn), lambda i, j: (i, j)),
        out_shape=jax.ShapeDtypeStruct((m, D_MODEL), BF16),
        compiler_params=_params(("parallel", "arbitrary")),
        name="merge_branches",
    )(a_conv, o_nsa, o_hg, w_conv_out, w_proj_nsa, w_proj_hgrn, z, z, z)


def _moe_kernel(u_ref, comb_ref, wg_ref, wu_ref, wd_ref, o_ref):
    e = pl.program_id(1)

    @pl.when(e == 0)
    def _():
        o_ref[...] = jnp.zeros_like(o_ref)

    u = u_ref[...]
    a = _bdot(u, wg_ref[0])
    b = _bdot(u, wu_ref[0])
    comb = comb_ref[...]
    lane = lax.broadcasted_iota(jnp.int32, comb.shape, 1)
    c = jnp.sum(jnp.where(lane == e, comb, 0.0), axis=1, keepdims=True)
    h = _silu(a) * b * c
    o_ref[...] += _bdot(h, wd_ref[0])


def moe_experts(u, comb, w_gate, w_up, w_down, tm=1024):
    m, d = u.shape
    tm = min(tm, m)
    return pl.pallas_call(
        _moe_kernel,
        grid=(m // tm, N_EXPERTS),
        in_specs=[pl.BlockSpec((tm, d), lambda i, e: (i, 0)),
                  pl.BlockSpec((tm, ROUTE_W), lambda i, e: (i, 0)),
                  pl.BlockSpec((1, d, D_EXPERT), lambda i, e: (e, 0, 0)),
                  pl.BlockSpec((1, d, D_EXPERT), lambda i, e: (e, 0, 0)),
                  pl.BlockSpec((1, D_EXPERT, d), lambda i, e: (e, 0, 0))],
        out_specs=pl.BlockSpec((tm, d), lambda i, e: (i, 0)),
        out_shape=jax.ShapeDtypeStruct((m, d), F32),
        compiler_params=_params(("parallel", "arbitrary")),
        name="moe_experts",
    )(u, comb, w_gate, w_up, w_down)


def _compress_kernel(x_ref, pa_ref, pb_ref, wa_ref, wb_ref, b1_ref, w2_ref, o_ref):
    x = x_ref[0]
    rows = x.shape[0]
    ha = _bdot(x + pa_ref[0], wa_ref[0])
    hb = _bdot(x + pb_ref[0], wb_ref[0])
    hb_next = pltpu.roll(hb, rows - 1, 0)
    h = _silu(ha + hb_next + b1_ref[0])
    o_ref[0] = _bdot(h, w2_ref[0])


def nsa_compress_blocks(x, pe, w1, b1, w2):
    rows = x.shape[1]
    kd = D_CMP * HEAD_DIM
    pa = pe[:, :D_CMP].reshape(2, 1, kd)
    pb = pe[:, D_CMP:].reshape(2, 1, kd)
    wa = w1[:, :D_CMP].reshape(2, kd, CMP_HID)
    wb = w1[:, D_CMP:].reshape(2, kd, CMP_HID)
    s_of = lambda i: (i % 2, 0, 0)
    return pl.pallas_call(
        _compress_kernel,
        grid=(2 * N_KV,),
        in_specs=[pl.BlockSpec((1, rows, kd), lambda i: (i, 0, 0)),
                  pl.BlockSpec((1, 1, kd), s_of), pl.BlockSpec((1, 1, kd), s_of),
                  pl.BlockSpec((1, kd, CMP_HID), s_of), pl.BlockSpec((1, kd, CMP_HID), s_of),
                  pl.BlockSpec((1, 1, CMP_HID), s_of), pl.BlockSpec((1, CMP_HID, HEAD_DIM), s_of)],
        out_specs=pl.BlockSpec((1, rows, HEAD_DIM), lambda i: (i, 0, 0)),
        out_shape=jax.ShapeDtypeStruct((2 * N_KV, rows, HEAD_DIM), F32),
        compiler_params=_params(("parallel",)),
        name="nsa_compress",
    )(x, pa, pb, wa, wb, b1.reshape(2, 1, CMP_HID), w2)


def _cmp_kernel(qT_ref, kc_ref, vcT_ref, ov_ref, o_ref, bias_ref, *, tq, k_sel):
    qb = pl.program_id(1)
    kc = kc_ref[0]
    vcT = vcT_ref[0]
    nblk = kc.shape[0]
    n_slc = ov_ref.shape[0]
    tpos = qb * tq + lax.broadcasted_iota(jnp.int32, (nblk, tq), 1)
    last = lax.broadcasted_iota(jnp.int32, (nblk, tq), 0) * D_CMP + (L_CMP - 1)
    vis = last <= tpos
    visf = vis.astype(F32)
    psum = jnp.zeros((nblk, tq), F32)
    for r in range(GRP):
        s = jnp.dot(kc, qT_ref[r], preferred_element_type=F32)
        s = jnp.where(vis, s, NEG)
        e = jnp.exp(s - jnp.max(s, axis=0, keepdims=True))
        p = e / jnp.sum(e, axis=0, keepdims=True) * visf
        o_ref[r] = jnp.dot(vcT, p.astype(BF16), preferred_element_type=F32)
        psum = psum + p
    p_hi, p_lo = _split_bf16(psum)
    ov = ov_ref[...]
    imp = jnp.dot(ov, p_hi, preferred_element_type=F32) + jnp.dot(ov, p_lo, preferred_element_type=F32)
    j = lax.broadcasted_iota(jnp.int32, (n_slc, tq), 0)
    cur = (qb * tq + lax.broadcasted_iota(jnp.int32, (n_slc, tq), 1)) // BLK_SLC
    valid = j <= cur
    forced = (j == 0) | (j == cur) | (j == cur - 1)
    score = jnp.where(forced, BIG, jnp.where(valid, imp, -BIG))
    jf = j.astype(F32)
    sel = jnp.zeros((n_slc, tq), F32)
    for _ in range(k_sel):
        m = jnp.max(score, axis=0, keepdims=True)
        idx = jnp.min(jnp.where(score == m, jf, 1e9), axis=0, keepdims=True)
        hit = jf == idx
        sel = jnp.where(hit, 1.0, sel)
        score = jnp.where(hit, -3e38, score)
    bias_ref[0] = jnp.where((sel > 0.0) & valid, 0.0, NEG).astype(BF16)


def _online_softmax_step(s, v, r, m_s, l_s, acc_s):
    m_old = m_s[r]
    m_new = jnp.maximum(m_old, jnp.max(s, axis=0, keepdims=True))
    alpha = jnp.exp(m_old - m_new)
    p = jnp.exp(s - m_new)
    l_s[r] = alpha * l_s[r] + jnp.sum(p, axis=0, keepdims=True)
    acc_s[r] = alpha * acc_s[r] + jnp.dot(v, p.astype(BF16), preferred_element_type=F32)
    m_s[r] = m_new


def _slc_kernel(qT_ref, bias_ref, k_ref, vT_ref, o_ref, rhs_s, m_s, l_s, acc_s, *, tq):
    qb = pl.program_id(1)
    m_s[...] = jnp.full(m_s.shape, NEG, F32)
    l_s[...] = jnp.zeros(l_s.shape, F32)
    acc_s[...] = jnp.zeros(acc_s.shape, F32)
    for r in range(GRP):
        rhs_s[r, 0:HEAD_DIM, :] = qT_ref[r]
        rhs_s[r, HEAD_DIM:, :] = bias_ref[0]

    def tile(kt, diagonal):
        k = k_ref[0, kt]
        v = vT_ref[0, kt]
        if diagonal:
            kpos = kt * tq + lax.broadcasted_iota(jnp.int32, (tq, tq), 0)
            tpos = qb * tq + lax.broadcasted_iota(jnp.int32, (tq, tq), 1)
            causal = kpos <= tpos
        for r in range(GRP):
            s = jnp.dot(k, rhs_s[r], preferred_element_type=F32)
            if diagonal:
                s = jnp.where(causal, s, NEG)
            _online_softmax_step(s, v, r, m_s, l_s, acc_s)

    def body(kt, carry):
        tile(kt, False)
        return carry

    lax.fori_loop(0, qb, body, 0)
    tile(qb, True)
    for r in range(GRP):
        o_ref[r] = acc_s[r] / l_s[r]


def _win_kernel(qT_ref, k_ref, vT_ref, oc_ref, os_ref, zng_ref, o_ref, m_s, l_s, acc_s, *, tq):
    qb = pl.program_id(1)
    m_s[...] = jnp.full(m_s.shape, NEG, F32)
    l_s[...] = jnp.zeros(l_s.shape, F32)
    acc_s[...] = jnp.zeros(acc_s.shape, F32)
    tpos = qb * tq + lax.broadcasted_iota(jnp.int32, (tq, tq), 1)
    row = lax.broadcasted_iota(jnp.int32, (tq, tq), 0)
    for i in range(WINDOW // tq + 1):
        kt_raw = qb - i
        kt = jnp.maximum(kt_raw, 0)
        k = k_ref[0, kt]
        v = vT_ref[0, kt]
        kpos = kt_raw * tq + row
        ok = (kpos <= tpos) & (kpos > tpos - WINDOW) & (kpos >= 0)
        for r in range(GRP):
            s = jnp.dot(k, qT_ref[r], preferred_element_type=F32)
            _online_softmax_step(jnp.where(ok, s, NEG), v, r, m_s, l_s, acc_s)
    zng = zng_ref[0]
    for r in range(GRP):
        o_win = acc_s[r] / l_s[r]
        g_cmp = _sigmoid(zng[r:r + 1, :])
        g_slc = _sigmoid(zng[GRP + r:GRP + r + 1, :])
        g_win = _sigmoid(zng[2 * GRP + r:2 * GRP + r + 1, :])
        o_ref[r] = (g_cmp * oc_ref[r] + g_slc * os_ref[r] + g_win * o_win).astype(o_ref.dtype)


def _tiles_rows(a, tq):
    g, t, c = a.shape
    return a.reshape(g, t // tq, tq, c)


def _tiles_cols(a, tq):
    g, c, t = a.shape
    return a.reshape(g, c, t // tq, tq).transpose(0, 2, 1, 3)


def nsa_prompt(z, pe, w1, b1, w2, tq=256):
    t = z.shape[0]
    nblk = t // D_CMP
    n_slc = t // BLK_SLC
    k_sel = min(TOP_N, n_slc)
    nq = t // tq
    zq = z[:, OFF_Q:OFF_Q + N_HEADS * HEAD_DIM]
    kv6 = z[:, OFF_KV:OFF_KV + N_KV * 6 * HEAD_DIM].reshape(t, N_KV, 6, HEAD_DIM)
    zng = z[:, OFF_NG:OFF_NG + 3 * N_HEADS]
    qT = (zq.reshape(t, N_HEADS, HEAD_DIM).transpose(1, 2, 0) * (HEAD_DIM ** -0.5)).astype(BF16)
    x = kv6[:, :, 0:2].reshape(nblk, D_CMP, N_KV, 2, HEAD_DIM).transpose(2, 3, 0, 1, 4)
    cmp = nsa_compress_blocks(x.reshape(2 * N_KV, nblk, D_CMP * HEAD_DIM), pe, w1, b1, w2)
    kc = cmp[0::2].astype(BF16)
    vcT = cmp[1::2].transpose(0, 2, 1).astype(BF16)
    i0 = np.arange(nblk)[None, :] * D_CMP
    j0 = np.arange(n_slc)[:, None] * BLK_SLC
    ovT = jnp.asarray(((i0 < j0 + BLK_SLC) & (i0 + L_CMP > j0)).astype(np.float32), dtype=BF16)

    head_blk = pl.BlockSpec((GRP, HEAD_DIM, tq), lambda g, i: (g, 0, i))
    scratch = [pltpu.VMEM((GRP, 1, tq), F32), pltpu.VMEM((GRP, 1, tq), F32), pltpu.VMEM((GRP, HEAD_DIM, tq), F32)]
    o_cmp, bias = pl.pallas_call(
        functools.partial(_cmp_kernel, tq=tq, k_sel=k_sel),
        grid=(N_KV, nq),
        in_specs=[head_blk,
                  pl.BlockSpec((1, nblk, HEAD_DIM), lambda g, i: (g, 0, 0)),
                  pl.BlockSpec((1, HEAD_DIM, nblk), lambda g, i: (g, 0, 0)),
                  pl.BlockSpec((n_slc, nblk), lambda g, i: (0, 0))],
        out_specs=[head_blk, pl.BlockSpec((1, n_slc, tq), lambda g, i: (g, 0, i))],
        out_shape=[jax.ShapeDtypeStruct((N_HEADS, HEAD_DIM, t), F32),
                   jax.ShapeDtypeStruct((N_KV, n_slc, t), BF16)],
        compiler_params=_params(("parallel", "arbitrary")),
        name="nsa_cmp_select",
    )(qT, kc, vcT, ovT)

    k_slc = kv6[:, :, 2].transpose(1, 0, 2)
    onehot = (jnp.arange(t)[:, None] // BLK_SLC == jnp.arange(n_slc)[None, :]).astype(F32)
    k_aug = jnp.concatenate([k_slc, jnp.broadcast_to(onehot, (N_KV, t, n_slc))], axis=-1).astype(BF16)
    v_slcT = kv6[:, :, 3].transpose(1, 2, 0).astype(BF16)
    ka = HEAD_DIM + n_slc
    o_slc = pl.pallas_call(
        functools.partial(_slc_kernel, tq=tq),
        grid=(N_KV, nq),
        in_specs=[head_blk,
                  pl.BlockSpec((1, n_slc, tq), lambda g, i: (g, 0, i)),
                  pl.BlockSpec((1, nq, tq, ka), lambda g, i: (g, 0, 0, 0)),
                  pl.BlockSpec((1, nq, HEAD_DIM, tq), lambda g, i: (g, 0, 0, 0))],
        out_specs=head_blk,
        out_shape=jax.ShapeDtypeStruct((N_HEADS, HEAD_DIM, t), F32),
        scratch_shapes=[pltpu.VMEM((GRP, ka, tq), BF16)] + scratch,
        compiler_params=_params(("parallel", "arbitrary")),
        name="nsa_selected",
    )(qT, bias, _tiles_rows(k_aug, tq), _tiles_cols(v_slcT, tq))

    k_win = kv6[:, :, 4].transpose(1, 0, 2).astype(BF16)
    v_winT = kv6[:, :, 5].transpose(1, 2, 0).astype(BF16)
    zngT = zng.reshape(t, N_KV, GRP, 3).transpose(1, 3, 2, 0).reshape(N_KV, 3 * GRP, t)
    oT = pl.pallas_call(
        functools.partial(_win_kernel, tq=tq),
        grid=(N_KV, nq),
        in_specs=[head_blk,
                  pl.BlockSpec((1, nq, tq, HEAD_DIM), lambda g, i: (g, 0, 0, 0)),
                  pl.BlockSpec((1, nq, HEAD_DIM, tq), lambda g, i: (g, 0, 0, 0)),
                  head_blk, head_blk,
                  pl.BlockSpec((1, 3 * GRP, tq), lambda g, i: (g, 0, i))],
        out_specs=head_blk,
        out_shape=jax.ShapeDtypeStruct((N_HEADS, HEAD_DIM, t), BF16),
        scratch_shapes=scratch,
        compiler_params=_params(("parallel", "arbitrary")),
        name="nsa_window_combine",
    )(qT, _tiles_rows(k_win, tq), _tiles_cols(v_winT, tq), o_cmp, o_slc, zngT)
    o = oT.transpose(2, 0, 1).reshape(t, N_HEADS * HEAD_DIM)
    return o, kv6


def _layer_norm(x, g, b):
    mu = jnp.mean(x, -1, keepdims=True)
    var = jnp.mean(jnp.square(x - mu), -1, keepdims=True)
    return (x - mu) * lax.rsqrt(var + LN_EPS) * g + b


def _conv_pre(z_conv, prev, w_dw, b_dw, ln_g, ln_b):
    a, g = jnp.split(z_conv, 2, axis=-1)
    u = a * jax.nn.sigmoid(g)
    padded = jnp.concatenate([prev, u], axis=1)
    y = lax.conv_general_dilated(padded, w_dw[:, None, :], window_strides=(1,), padding='VALID',
                                 dimension_numbers=('NWC', 'WIO', 'NWC'), feature_group_count=D_CONV) + b_dw
    return jax.nn.silu(_layer_norm(y, ln_g, ln_b)), padded[:, -(CONV_W - 1):]


def _gated_recurrence(q, k, v, logf, s0):
    B, T, H, DK = q.shape
    C = min(HG_CHUNK, T)
    n = T // C

    def blk(a):
        return jnp.moveaxis(a.reshape(B, n, C, *a.shape[2:]), 1, 0)

    tri = jnp.tril(jnp.ones((C, C), bool))[None, :, :, None, None]

    def step(S, xs):
        qc, kc, vc, gc = xs
        G = jnp.cumsum(gc, axis=1)
        o_inter = jnp.einsum('bchk,bhkv->bchv', qc * jnp.exp(G), S)
        decay = jnp.exp(jnp.where(tri, G[:, :, None] - G[:, None, :], -jnp.inf))
        A = jnp.einsum('bthk,bshk,btshk->bths', qc, kc, decay)
        o_intra = jnp.einsum('bths,bshv->bthv', A, vc)
        G_last = G[:, -1]
        S_new = jnp.exp(G_last)[..., None] * S + jnp.einsum('bshk,bshv->bhkv', kc * jnp.exp(G_last[:, None] - G), vc)
        return S_new, o_inter + o_intra

    s_fin, o = lax.scan(step, s0, (blk(q), blk(k), blk(v), blk(logf)))
    return jnp.moveaxis(o, 0, 1).reshape(B, T, H, v.shape[-1]), s_fin


def _hgrn_pre(zh, s0, lb, norm_g):
    B, T = zh.shape[:2]
    zq, zf, zi, zg = jnp.split(zh, 4, axis=-1)
    q = jax.nn.silu(zq).reshape(B, T, H_HG, DK_HG)
    lbh = lb.reshape(H_HG, DK_HG)
    f = lbh + (1.0 - lbh) * jax.nn.sigmoid(zf).reshape(B, T, H_HG, DK_HG)
    v = zi.reshape(B, T, H_HG, DV_HG)
    o, s_new = _gated_recurrence(q, 1.0 - f, v, jnp.log(f), s0)
    o = o * lax.rsqrt(jnp.mean(jnp.square(o), -1, keepdims=True) + LN_EPS) * norm_g.reshape(H_HG, DV_HG)
    o = o * jax.nn.silu(zg).reshape(B, T, H_HG, DV_HG)
    return o.reshape(B, T, H_HG * DV_HG), s_new


def _nsa_compress_jax(kv, pe, w1, b1, w2):
    B, L = kv.shape[:2]
    n16 = L // D_CMP
    x = kv[:, :n16 * D_CMP].reshape(B, n16, D_CMP, N_KV, 2, HEAD_DIM)
    pe_t = jnp.transpose(pe, (1, 0, 2))
    ha = jnp.einsum('bnjgsd,sjdh->bngsh', x + pe_t[None, None, :D_CMP, None], w1[:, :D_CMP])
    hb = jnp.einsum('bnjgsd,sjdh->bngsh', x + pe_t[None, None, D_CMP:, None], w1[:, D_CMP:])
    h = jax.nn.silu(ha[:, :-1] + hb[:, 1:] + b1)
    return jnp.einsum('bngsh,shd->bngsd', h, w2)


def _nsa_attend_jax(q, gates, qpos, cmp_kv, slc_kv, win_kv, wpos):
    scale = HEAD_DIM ** -0.5
    B, Tq = q.shape[:2]
    L = slc_kv.shape[1]
    n_cmp = cmp_kv.shape[1]
    cmp_last = jnp.arange(n_cmp) * D_CMP + L_CMP - 1
    vis = (cmp_last[None, :] <= qpos[:, None])[None, :, None, None, :]
    s = jnp.einsum('btgrd,bigd->btgri', q, cmp_kv[..., 0, :]) * scale
    p_cmp = jax.nn.softmax(jnp.where(vis, s, NEG), axis=-1) * vis
    o_cmp = jnp.einsum('btgri,bigd->btgrd', p_cmp, cmp_kv[..., 1, :])
    n_slc = -(-L // BLK_SLC)
    i_start = jnp.arange(n_cmp) * D_CMP
    j_start = jnp.arange(n_slc) * BLK_SLC
    overlap = ((i_start[:, None] < j_start[None] + BLK_SLC) & (i_start[:, None] + L_CMP > j_start[None])).astype(F32)
    imp = jnp.einsum('btgri,ij->btgj', p_cmp, overlap)
    cur = qpos // BLK_SLC
    jj = jnp.arange(n_slc)[None]
    valid = (jj <= cur[:, None])[None, :, None]
    forced = ((jj == 0) | (jj == cur[:, None]) | (jj == cur[:, None] - 1))[None, :, None]
    score = jnp.where(forced, BIG, jnp.where(valid, imp, -BIG))
    k_sel = min(TOP_N, n_slc)
    _, idx = lax.top_k(score, k_sel)
    pos = (idx[..., None] * BLK_SLC + jnp.arange(BLK_SLC)).reshape(B, Tq, N_KV, k_sel * BLK_SLC)
    ok = (pos <= qpos[None, :, None, None])[:, :, :, None, :]
    bi = jnp.arange(B)[:, None, None, None]
    gi = jnp.arange(N_KV)[None, None, :, None]
    kv_sel = slc_kv[bi, jnp.minimum(pos, L - 1), gi]
    s = jnp.einsum('btgrd,btgkd->btgrk', q, kv_sel[..., 0, :]) * scale
    p = jax.nn.softmax(jnp.where(ok, s, NEG), axis=-1)
    o_slc = jnp.einsum('btgrk,btgkd->btgrd', p, kv_sel[..., 1, :])
    wok = ((wpos[None] <= qpos[:, None]) & (wpos[None] > qpos[:, None] - WINDOW) & (wpos[None] >= 0))[None, :, None, None, :]
    s = jnp.einsum('btgrd,bsgd->btgrs', q, win_kv[..., 0, :]) * scale
    p = jax.nn.softmax(jnp.where(wok, s, NEG), axis=-1)
    o_win = jnp.einsum('btgrs,bsgd->btgrd', p, win_kv[..., 1, :])
    return gates[..., 0:1] * o_cmp + gates[..., 1:2] * o_slc + gates[..., 2:3] * o_win


def _nsa_sample_jax(zq, zng, kv6, pool_l, page_table, win_buf, pe, w1, b1, w2):
    B, T = zq.shape[:2]
    past = page_table.shape[1] * PAGE_SIZE
    gates = jax.nn.sigmoid(zng)
    past_cmp = pool_l[page_table, :, :, 0:2].reshape(B, past, N_KV, 2, HEAD_DIM)
    past_slc = pool_l[page_table, :, :, 2:4].reshape(B, past, N_KV, 2, HEAD_DIM)
    full_cmp = jnp.concatenate([past_cmp, kv6[..., 0:2, :]], axis=1)
    full_slc = jnp.concatenate([past_slc, kv6[..., 2:4, :]], axis=1)
    cmp_kv = _nsa_compress_jax(full_cmp, pe, w1, b1, w2)
    wlen = win_buf.shape[1]
    wkv = jnp.concatenate([win_buf, kv6[..., 4:6, :]], axis=1)
    wpos = past - wlen + jnp.arange(wlen + T, dtype=jnp.int32)
    qpos = past + jnp.arange(T, dtype=jnp.int32)
    o = _nsa_attend_jax(zq.reshape(B, T, N_KV, GRP, HEAD_DIM), gates.reshape(B, T, N_KV, GRP, 3), qpos,
                        cmp_kv, full_slc, wkv, wpos)
    return o.reshape(B, T, N_HEADS * HEAD_DIM), wkv[:, -wlen:]


def _relayout_w_in(w):
    o_conv, o_q, o_kv, o_ng = 0, 2048, 3072, 4608
    o_h, o_mg, end = 4656, 8752, 14896
    parts = [w[:, o_mg:end], w[:, o_conv:o_q], w[:, o_q:o_kv], w[:, o_kv:o_ng], w[:, o_h:o_mg], w[:, o_ng:o_h],
             jnp.zeros((w.shape[0], NZ - OFF_NG - 3 * N_HEADS), w.dtype)]
    return jnp.concatenate(parts, axis=1).astype(BF16)


def _route_weights(wg, bg, we, be):
    pad = ROUTE_W - N_EXPERTS - N_GROUPS
    w = jnp.concatenate([we, wg, jnp.zeros((we.shape[0], pad), F32)], axis=1)
    b = jnp.concatenate([be, bg, jnp.zeros((pad,), F32)])[None, :]
    return w, b


def _layer(x, mods, P, l, lb, conv_prev, s0, batch, nsa_state):
    m = x.shape[0]
    t = m // batch
    (sh1, sc1, g1), (sh2, sc2, g2) = mods
    u = modulate(x, sc1, sh1)
    z = matmul(u, P['w_in_r'][l], tn=NZ_TILE)
    z3 = z.reshape(batch, t, NZ)
    a_conv, conv_new = _conv_pre(z3[..., OFF_CONV:OFF_CONV + 2 * D_CONV], conv_prev, P['w_dw'][l], P['b_dw'][l],
                                 P['conv_ln_g'][l], P['conv_ln_b'][l])
    o_hg, s_new = _hgrn_pre(z3[..., OFF_HG:OFF_HG + 4 * H_HG * DK_HG], s0, lb, P['hgrn_norm_g'][l])
    cmp_w = (P['w_cmp_pe'][l], P['w_cmp_1'][l], P['b_cmp_1'][l], P['w_cmp_2'][l])
    if nsa_state is None:
        o_nsa, kv6 = nsa_prompt(z, *cmp_w)
        kv6 = kv6[None]
        win_new = kv6[:, -min(WINDOW, t):, :, 4:6]
    else:
        pool_l, page_table, win_buf = nsa_state
        kv6 = z3[..., OFF_KV:OFF_KV + N_KV * 6 * HEAD_DIM].reshape(batch, t, N_KV, 6, HEAD_DIM)
        o_nsa, win_new = _nsa_sample_jax(z3[..., OFF_Q:OFF_Q + N_HEADS * HEAD_DIM], z3[..., OFF_NG:OFF_NG + 3 * N_HEADS],
                                         kv6, pool_l, page_table, win_buf, *cmp_w)
        o_nsa = o_nsa.reshape(m, -1)
    merged = merge_branches(a_conv.reshape(m, D_CONV), o_nsa, o_hg.reshape(m, -1),
                            P['w_conv_out'][l], P['w_proj_nsa'][l], P['w_proj_hgrn'][l], z)
    y = matmul(merged, P['w_out'][l], tn=512)
    w_r, b_r = _route_weights(P['w_route_group'][l], P['b_route_group'][l], P['w_route_expert'][l], P['b_route_expert'][l])
    x1, u2, comb = ln_residual_route(x, y, g1, P['ln_g'][l, 0][None], P['ln_b'][l, 0][None], sc2, sh2, w_r, b_r)
    y2 = moe_experts(u2, comb, P['w_exp_gate'][l], P['w_exp_up'][l], P['w_exp_down'][l])
    x2 = ln_residual(x1, y2, g2, P['ln_g'][l, 1][None], P['ln_b'][l, 1][None])
    return x2, (kv6[..., 0:4, :], win_new, conv_new, s_new)


def kernel(x_prompt, x_sample, cache_nsa_kv, state_win_kv, state_conv, state_hgrn, page_table,
           c_prompt, c_sample, w_ada, b_ada, w_in, w_cmp_pe, w_cmp_1, b_cmp_1, w_cmp_2,
           w_dw, b_dw, conv_ln_g, conv_ln_b, w_conv_out, hgrn_lb_logits, hgrn_norm_g,
           w_proj_nsa, w_proj_hgrn, w_out, ln_g, ln_b, w_route_group, b_route_group,
           w_route_expert, b_route_expert, w_exp_gate, w_exp_up, w_exp_down):
    P = dict(w_cmp_pe=w_cmp_pe, w_cmp_1=w_cmp_1, b_cmp_1=b_cmp_1, w_cmp_2=w_cmp_2, w_dw=w_dw, b_dw=b_dw,
             conv_ln_g=conv_ln_g, conv_ln_b=conv_ln_b, w_conv_out=w_conv_out, hgrn_norm_g=hgrn_norm_g,
             w_proj_nsa=w_proj_nsa, w_proj_hgrn=w_proj_hgrn, w_out=w_out, ln_g=ln_g, ln_b=ln_b,
             w_route_group=w_route_group, b_route_group=b_route_group, w_route_expert=w_route_expert,
             b_route_expert=b_route_expert, w_exp_gate=w_exp_gate, w_exp_up=w_exp_up, w_exp_down=w_exp_down)
    P['w_in_r'] = [_relayout_w_in(w_in[l]) for l in range(DEPTH)]
    p_lb = jax.nn.softmax(hgrn_lb_logits, axis=0)
    lb_all = jnp.cumsum(p_lb, axis=0) - p_lb[0:1]

    bp, tp = x_prompt.shape[:2]
    bs, ts = x_sample.shape[:2]
    c_all = jnp.concatenate([c_sample, c_prompt, jnp.zeros((8 - bp % 8, D_MODEL), F32)], axis=0)
    ada = ada_all(c_all, w_ada, b_ada)

    def mods(l, i, rows, rep):
        mrow = ada[2 * l + i, rows]
        if rep > 1:
            mrow = jnp.repeat(mrow, rep, axis=0)
        return mrow[:, :D_MODEL], mrow[:, D_MODEL:2 * D_MODEL], mrow[:, 2 * D_MODEL:]

    xp = x_prompt.reshape(bp * tp, D_MODEL)
    xs = x_sample.reshape(bs * ts, D_MODEL)
    outs_p, outs_s = [], []
    for l in range(DEPTH):
        conv0 = jnp.zeros((bp, CONV_W - 1, D_CONV), F32)
        s0 = jnp.zeros((bp, H_HG, DK_HG, DV_HG), F32)
        mp = [mods(l, i, slice(bs, bs + bp), 1) for i in range(2)]
        xp, st = _layer(xp, mp, P, l, lb_all[l], conv0, s0, bp, None)
        outs_p.append(st)
        ms = [mods(l, i, slice(0, bs), ts) for i in range(2)]
        xs, st = _layer(xs, ms, P, l, lb_all[l], state_conv[l], state_hgrn[l], bs,
                        (cache_nsa_kv[l], page_table, state_win_kv[l]))
        outs_s.append(st)
    stack = lambda outs, i: jnp.stack([o[i] for o in outs])
    return (xp.reshape(bp, tp, D_MODEL), xs.reshape(bs, ts, D_MODEL),
            stack(outs_p, 0), stack(outs_p, 1), stack(outs_p, 2), stack(outs_p, 3),
            stack(outs_s, 0), stack(outs_s, 1), stack(outs_s, 2), stack(outs_s, 3))
```

```python
import functools

import numpy as np
import jax
import jax.numpy as jnp
from jax import lax
from jax.experimental import pallas as pl
from jax.experimental.pallas import tpu as pltpu

F32 = jnp.float32
BF16 = jnp.bfloat16

D_MODEL = 2048
DEPTH = 2
D_CONV = 1024
CONV_W = 31
N_HEADS = 16
N_KV = 4
HEAD_DIM = 64
GRP = N_HEADS // N_KV
L_CMP = 32
D_CMP = 16
CMP_HID = 256
BLK_SLC = 64
TOP_N = 16
WINDOW = 512
H_HG = 8
DK_HG = 128
DV_HG = 128
HG_CHUNK = 64
N_GROUPS = 4
EXP_PER_GROUP = 8
N_EXPERTS = N_GROUPS * EXP_PER_GROUP
D_EXPERT = 256
PAGE_SIZE = 128

ALPHA = (2 * DEPTH) ** 0.25
LN_EPS = 1e-5
NEG = -1e30
BIG = 1e9

OFF_MG = 0
OFF_CONV = 3 * D_MODEL
OFF_Q = OFF_CONV + 2 * D_CONV
OFF_HG = OFF_Q + N_HEADS * HEAD_DIM
OFF_KV = OFF_HG + 4 * H_HG * DK_HG
OFF_NG = OFF_KV + N_KV * 6 * HEAD_DIM
NZ = OFF_NG + 128
NZ_TILE = 1152
ROUTE_W = 128


def _params(sem, vmem_mb=48):
    return pltpu.CompilerParams(dimension_semantics=sem, vmem_limit_bytes=vmem_mb * 1024 * 1024)


def _sigmoid(x):
    return 1.0 / (1.0 + jnp.exp(-x))


def _silu(x):
    return x * _sigmoid(x)


def _bdot(a, b):
    return jnp.dot(a.astype(BF16), b.astype(BF16), preferred_element_type=F32)


def _row_spec(arr, tm):
    d = arr.shape[1]
    if arr.shape[0] == 1:
        return pl.BlockSpec((1, d), lambda i: (0, 0))
    return pl.BlockSpec((tm, d), lambda i: (i, 0))


def _mm_kernel(x_ref, w_ref, o_ref):
    o_ref[...] = _bdot(x_ref[...], w_ref[...]).astype(o_ref.dtype)


def matmul(x, w, *, tn, out_dtype=F32, tm=1024):
    m, k = x.shape
    n = w.shape[1]
    tm = min(tm, m)
    return pl.pallas_call(
        _mm_kernel,
        grid=(m // tm, n // tn),
        in_specs=[pl.BlockSpec((tm, k), lambda i, j: (i, 0)),
                  pl.BlockSpec((k, tn), lambda i, j: (0, j))],
        out_specs=pl.BlockSpec((tm, tn), lambda i, j: (i, j)),
        out_shape=jax.ShapeDtypeStruct((m, n), out_dtype),
        compiler_params=_params(("parallel", "arbitrary")),
        name="matmul",
    )(x, w)


def _ada_kernel(c_ref, w_ref, b_ref, o_ref, *, tn):
    j = pl.program_id(1)
    m = _bdot(_silu(c_ref[...]), w_ref[0]) + b_ref[0]
    col = j * tn + lax.broadcasted_iota(jnp.int32, m.shape, 1)
    o_ref[0] = m + (col >= D_MODEL).astype(F32)


def ada_all(c, w_ada, b_ada, tn=1024):
    r = c.shape[0]
    n = 3 * D_MODEL
    w = w_ada.reshape(2 * DEPTH, D_MODEL, n)
    b = b_ada.reshape(2 * DEPTH, 1, n)
    return pl.pallas_call(
        functools.partial(_ada_kernel, tn=tn),
        grid=(2 * DEPTH, n // tn),
        in_specs=[pl.BlockSpec((r, D_MODEL), lambda a, j: (0, 0)),
                  pl.BlockSpec((1, D_MODEL, tn), lambda a, j: (a, 0, j)),
                  pl.BlockSpec((1, 1, tn), lambda a, j: (a, 0, j))],
        out_specs=pl.BlockSpec((1, r, tn), lambda a, j: (a, 0, j)),
        out_shape=jax.ShapeDtypeStruct((2 * DEPTH, r, n), F32),
        compiler_params=_params(("parallel", "arbitrary")),
        name="ada",
    )(c, w, b)


def _mod_kernel(x_ref, sc_ref, sh_ref, o_ref):
    o_ref[...] = (x_ref[...] * sc_ref[...] + sh_ref[...]).astype(o_ref.dtype)


def modulate(x, scale, shift, tm=512):
    m, d = x.shape
    tm = min(tm, m)
    return pl.pallas_call(
        _mod_kernel,
        grid=(m // tm,),
        in_specs=[pl.BlockSpec((tm, d), lambda i: (i, 0)), _row_spec(scale, tm), _row_spec(shift, tm)],
        out_specs=pl.BlockSpec((tm, d), lambda i: (i, 0)),
        out_shape=jax.ShapeDtypeStruct((m, d), BF16),
        compiler_params=_params(("parallel",)),
        name="modulate",
    )(x, scale, shift)


def _post_ln(x_ref, y_ref, gate_ref, g_ref, b_ref):
    v = ALPHA * x_ref[...] + gate_ref[...] * y_ref[...]
    mu = jnp.mean(v, axis=-1, keepdims=True)
    d = v - mu
    var = jnp.mean(d * d, axis=-1, keepdims=True)
    return d * lax.rsqrt(var + LN_EPS) * g_ref[...] + b_ref[...]


def _ln_kernel(x_ref, y_ref, gate_ref, g_ref, b_ref, o_ref):
    o_ref[...] = _post_ln(x_ref, y_ref, gate_ref, g_ref, b_ref)


def _split_bf16(a):
    hi = a.astype(BF16)
    lo = (a - hi.astype(F32)).astype(BF16)
    return hi, lo


def _route(logits):
    lane = lax.broadcasted_iota(jnp.int32, logits.shape, 1).astype(F32)
    far = jnp.float32(1e9)
    is_g = (lane >= N_EXPERTS) & (lane < N_EXPERTS + N_GROUPS)
    gl = jnp.where(is_g, logits, -jnp.inf)
    gmax = jnp.max(gl, axis=1, keepdims=True)
    gidx = jnp.min(jnp.where(gl == gmax, lane, far), axis=1, keepdims=True) - N_EXPERTS
    wg = 1.0 / jnp.sum(jnp.where(is_g, jnp.exp(gl - gmax), 0.0), axis=1, keepdims=True)
    lo = gidx * EXP_PER_GROUP
    in_g = (lane >= lo) & (lane < lo + EXP_PER_GROUP)
    el = jnp.where(in_g, logits, -jnp.inf)
    m1 = jnp.max(el, axis=1, keepdims=True)
    i1 = jnp.min(jnp.where(el == m1, lane, far), axis=1, keepdims=True)
    el2 = jnp.where(lane == i1, -jnp.inf, el)
    m2 = jnp.max(el2, axis=1, keepdims=True)
    i2 = jnp.min(jnp.where(el2 == m2, lane, far), axis=1, keepdims=True)
    e2 = jnp.exp(m2 - m1)
    t1 = 1.0 / (1.0 + e2)
    t2 = e2 / (1.0 + e2)
    return jnp.where(lane == i1, wg * t1, 0.0) + jnp.where(lane == i2, wg * t2, 0.0)


def _ln_route_kernel(x_ref, y_ref, gate_ref, g_ref, b_ref, sc_ref, sh_ref, wr_ref, br_ref,
                     xn_ref, u_ref, comb_ref):
    xn = _post_ln(x_ref, y_ref, gate_ref, g_ref, b_ref)
    xn_ref[...] = xn
    u = xn * sc_ref[...] + sh_ref[...]
    u_ref[...] = u.astype(BF16)
    u_hi, u_lo = _split_bf16(u)
    w_hi, w_lo = _split_bf16(wr_ref[...])
    logits = (jnp.dot(u_hi, w_hi, preferred_element_type=F32)
              + jnp.dot(u_lo, w_hi, preferred_element_type=F32)
              + jnp.dot(u_hi, w_lo, preferred_element_type=F32)) + br_ref[...]
    comb_ref[...] = _route(logits)


def ln_residual(x, y, gate, g, b, tm=512):
    m, d = x.shape
    tm = min(tm, m)
    row = pl.BlockSpec((tm, d), lambda i: (i, 0))
    vec = pl.BlockSpec((1, d), lambda i: (0, 0))
    return pl.pallas_call(
        _ln_kernel,
        grid=(m // tm,),
        in_specs=[row, row, _row_spec(gate, tm), vec, vec],
        out_specs=row,
        out_shape=jax.ShapeDtypeStruct((m, d), F32),
        compiler_params=_params(("parallel",)),
        name="ln_residual",
    )(x, y, gate, g, b)


def ln_residual_route(x, y, gate, g, b, scale, shift, w_route, b_route, tm=512):
    m, d = x.shape
    tm = min(tm, m)
    row = pl.BlockSpec((tm, d), lambda i: (i, 0))
    vec = pl.BlockSpec((1, d), lambda i: (0, 0))
    return pl.pallas_call(
        _ln_route_kernel,
        grid=(m // tm,),
        in_specs=[row, row, _row_spec(gate, tm), vec, vec, _row_spec(scale, tm), _row_spec(shift, tm),
                  pl.BlockSpec((d, ROUTE_W), lambda i: (0, 0)), pl.BlockSpec((1, ROUTE_W), lambda i: (0, 0))],
        out_specs=[row, row, pl.BlockSpec((tm, ROUTE_W), lambda i: (i, 0))],
        out_shape=[jax.ShapeDtypeStruct((m, d), F32), jax.ShapeDtypeStruct((m, d), BF16),
                   jax.ShapeDtypeStruct((m, ROUTE_W), F32)],
        compiler_params=_params(("parallel",)),
        name="ln_residual_route",
    )(x, y, gate, g, b, scale, shift, w_route, b_route)


def _merge_kernel(ac_ref, on_ref, oh_ref, wc_ref, wn_ref, wh_ref, za_ref, zb_ref, zc_ref, o_ref):
    yc = _bdot(ac_ref[...], wc_ref[...])
    yn = _bdot(on_ref[...], wn_ref[...])
    yh = _bdot(oh_ref[...], wh_ref[...])
    o = _sigmoid(za_ref[...]) * yc + _sigmoid(zb_ref[...]) * yn + _sigmoid(zc_ref[...]) * yh
    o_ref[...] = o.astype(o_ref.dtype)


def merge_branches(a_conv, o_nsa, o_hg, w_conv_out, w_proj_nsa, w_proj_hgrn, z, tm=512, tn=512):
    m, k = a_conv.shape
    tm = min(tm, m)
    nb = D_MODEL // tn
    act = pl.BlockSpec((tm, k), lambda i, j: (i, 0))
    wsp = pl.BlockSpec((k, tn), lambda i, j: (0, j))
    gate = [pl.BlockSpec((tm, tn), lambda i, j, o=(OFF_MG + c * D_MODEL) // tn: (i, o + j)) for c in range(3)]
    return pl.pallas_call(
        _merge_kernel,
        grid=(m // tm, nb),
        in_specs=[act, act, act, wsp, wsp, wsp] + gate,
        out_specs=pl.BlockSpec((tm, tn), lambda i, j: (i, j)),
        out_shape=jax.ShapeDtypeStruct((m, D_MODEL), BF16),
        compiler_params=_params(("parallel", "arbitrary")),
        name="merge_branches",
    )(a_conv, o_nsa, o_hg, w_conv_out, w_proj_nsa, w_proj_hgrn, z, z, z)


def _moe_kernel(u_ref, comb_ref, wg_ref, wu_ref, wd_ref, o_ref):
    e = pl.program_id(1)

    @pl.when(e == 0)
    def _():
        o_ref[...] = jnp.zeros_like(o_ref)

    u = u_ref[...]
    a = _bdot(u, wg_ref[0])
    b = _bdot(u, wu_ref[0])
    comb = comb_ref[...]
    lane = lax.broadcasted_iota(jnp.int32, comb.shape, 1)
    c = jnp.sum(jnp.where(lane == e, comb, 0.0), axis=1, keepdims=True)
    h = _silu(a) * b * c
    o_ref[...] += _bdot(h, wd_ref[0])


def moe_experts(u, comb, w_gate, w_up, w_down, tm=1024):
    m, d = u.shape
    tm = min(tm, m)
    return pl.pallas_call(
        _moe_kernel,
        grid=(m // tm, N_EXPERTS),
        in_specs=[pl.BlockSpec((tm, d), lambda i, e: (i, 0)),
                  pl.BlockSpec((tm, ROUTE_W), lambda i, e: (i, 0)),
                  pl.BlockSpec((1, d, D_EXPERT), lambda i, e: (e, 0, 0)),
                  pl.BlockSpec((1, d, D_EXPERT), lambda i, e: (e, 0, 0)),
                  pl.BlockSpec((1, D_EXPERT, d), lambda i, e: (e, 0, 0))],
        out_specs=pl.BlockSpec((tm, d), lambda i, e: (i, 0)),
        out_shape=jax.ShapeDtypeStruct((m, d), F32),
        compiler_params=_params(("parallel", "arbitrary")),
        name="moe_experts",
    )(u, comb, w_gate, w_up, w_down)


def _compress_kernel(x_ref, pa_ref, pb_ref, wa_ref, wb_ref, b1_ref, w2_ref, o_ref):
    x = x_ref[0]
    rows = x.shape[0]
    ha = _bdot(x + pa_ref[0], wa_ref[0])
    hb = _bdot(x + pb_ref[0], wb_ref[0])
    hb_next = pltpu.roll(hb, rows - 1, 0)
    h = _silu(ha + hb_next + b1_ref[0])
    o_ref[0] = _bdot(h, w2_ref[0])


def nsa_compress_blocks(x, pe, w1, b1, w2):
    rows = x.shape[1]
    kd = D_CMP * HEAD_DIM
    pa = pe[:, :D_CMP].reshape(2, 1, kd)
    pb = pe[:, D_CMP:].reshape(2, 1, kd)
    wa = w1[:, :D_CMP].reshape(2, kd, CMP_HID)
    wb = w1[:, D_CMP:].reshape(2, kd, CMP_HID)
    s_of = lambda i: (i % 2, 0, 0)
    return pl.pallas_call(
        _compress_kernel,
        grid=(2 * N_KV,),
        in_specs=[pl.BlockSpec((1, rows, kd), lambda i: (i, 0, 0)),
                  pl.BlockSpec((1, 1, kd), s_of), pl.BlockSpec((1, 1, kd), s_of),
                  pl.BlockSpec((1, kd, CMP_HID), s_of), pl.BlockSpec((1, kd, CMP_HID), s_of),
                  pl.BlockSpec((1, 1, CMP_HID), s_of), pl.BlockSpec((1, CMP_HID, HEAD_DIM), s_of)],
        out_specs=pl.BlockSpec((1, rows, HEAD_DIM), lambda i: (i, 0, 0)),
        out_shape=jax.ShapeDtypeStruct((2 * N_KV, rows, HEAD_DIM), F32),
        compiler_params=_params(("parallel",)),
        name="nsa_compress",
    )(x, pa, pb, wa, wb, b1.reshape(2, 1, CMP_HID), w2)


def _cmp_kernel(qT_ref, kc_ref, vcT_ref, ov_ref, o_ref, bias_ref, *, tq, k_sel):
    qb = pl.program_id(1)
    kc = kc_ref[0]
    vcT = vcT_ref[0]
    nblk = kc.shape[0]
    n_slc = ov_ref.shape[0]
    tpos = qb * tq + lax.broadcasted_iota(jnp.int32, (nblk, tq), 1)
    last = lax.broadcasted_iota(jnp.int32, (nblk, tq), 0) * D_CMP + (L_CMP - 1)
    vis = last <= tpos
    visf = vis.astype(F32)
    psum = jnp.zeros((nblk, tq), F32)
    for r in range(GRP):
        s = jnp.dot(kc, qT_ref[r], preferred_element_type=F32)
        s = jnp.where(vis, s, NEG)
        e = jnp.exp(s - jnp.max(s, axis=0, keepdims=True))
        p = e / jnp.sum(e, axis=0, keepdims=True) * visf
        o_ref[r] = jnp.dot(vcT, p.astype(BF16), preferred_element_type=F32)
        psum = psum + p
    p_hi, p_lo = _split_bf16(psum)
    ov = ov_ref[...]
    imp = jnp.dot(ov, p_hi, preferred_element_type=F32) + jnp.dot(ov, p_lo, preferred_element_type=F32)
    j = lax.broadcasted_iota(jnp.int32, (n_slc, tq), 0)
    cur = (qb * tq + lax.broadcasted_iota(jnp.int32, (n_slc, tq), 1)) // BLK_SLC
    valid = j <= cur
    forced = (j == 0) | (j == cur) | (j == cur - 1)
    score = jnp.where(forced, BIG, jnp.where(valid, imp, -BIG))
    jf = j.astype(F32)
    sel = jnp.zeros((n_slc, tq), F32)
    for _ in range(k_sel):
        m = jnp.max(score, axis=0, keepdims=True)
        idx = jnp.min(jnp.where(score == m, jf, 1e9), axis=0, keepdims=True)
        hit = jf == idx
        sel = jnp.where(hit, 1.0, sel)
        score = jnp.where(hit, -3e38, score)
    bias_ref[0] = jnp.where((sel > 0.0) & valid, 0.0, NEG).astype(BF16)


def _online_softmax_step(s, v, r, m_s, l_s, acc_s):
    m_old = m_s[r]
    m_new = jnp.maximum(m_old, jnp.max(s, axis=0, keepdims=True))
    alpha = jnp.exp(m_old - m_new)
    p = jnp.exp(s - m_new)
    l_s[r] = alpha * l_s[r] + jnp.sum(p, axis=0, keepdims=True)
    acc_s[r] = alpha * acc_s[r] + jnp.dot(v, p.astype(BF16), preferred_element_type=F32)
    m_s[r] = m_new


def _slc_kernel(qT_ref, bias_ref, k_ref, vT_ref, o_ref, rhs_s, m_s, l_s, acc_s, *, tq):
    qb = pl.program_id(1)
    m_s[...] = jnp.full(m_s.shape, NEG, F32)
    l_s[...] = jnp.zeros(l_s.shape, F32)
    acc_s[...] = jnp.zeros(acc_s.shape, F32)
    for r in range(GRP):
        rhs_s[0:HEAD_DIM, r * tq:(r + 1) * tq] = qT_ref[r]
        rhs_s[HEAD_DIM:, r * tq:(r + 1) * tq] = bias_ref[0]

    def tile(kt, diagonal):
        s = jnp.dot(k_ref[0, kt], rhs_s[...], preferred_element_type=F32)
        if diagonal:
            kpos = kt * tq + lax.broadcasted_iota(jnp.int32, (tq, GRP * tq), 0)
            tpos = qb * tq + lax.broadcasted_iota(jnp.int32, (tq, GRP * tq), 1) % tq
            s = jnp.where(kpos <= tpos, s, NEG)
        m_old = m_s[...]
        m_new = jnp.maximum(m_old, jnp.max(s, axis=0, keepdims=True))
        alpha = jnp.exp(m_old - m_new)
        p = jnp.exp(s - m_new)
        l_s[...] = alpha * l_s[...] + jnp.sum(p, axis=0, keepdims=True)
        acc_s[...] = alpha * acc_s[...] + jnp.dot(vT_ref[0, kt], p.astype(BF16), preferred_element_type=F32)
        m_s[...] = m_new

    def body(kt, carry):
        tile(kt, False)
        return carry

    lax.fori_loop(0, qb, body, 0)
    tile(qb, True)
    o = acc_s[...] / l_s[...]
    for r in range(GRP):
        o_ref[r] = o[:, r * tq:(r + 1) * tq]


def _win_kernel(qT_ref, k_ref, vT_ref, oc_ref, os_ref, zng_ref, o_ref, m_s, l_s, acc_s, *, tq):
    qb = pl.program_id(1)
    m_s[...] = jnp.full(m_s.shape, NEG, F32)
    l_s[...] = jnp.zeros(l_s.shape, F32)
    acc_s[...] = jnp.zeros(acc_s.shape, F32)
    tpos = qb * tq + lax.broadcasted_iota(jnp.int32, (tq, tq), 1)
    row = lax.broadcasted_iota(jnp.int32, (tq, tq), 0)
    for i in range(WINDOW // tq + 1):
        kt_raw = qb - i
        kt = jnp.maximum(kt_raw, 0)
        k = k_ref[0, kt]
        v = vT_ref[0, kt]
        kpos = kt_raw * tq + row
        ok = (kpos <= tpos) & (kpos > tpos - WINDOW) & (kpos >= 0)
        for r in range(GRP):
            s = jnp.dot(k, qT_ref[r], preferred_element_type=F32)
            _online_softmax_step(jnp.where(ok, s, NEG), v, r, m_s, l_s, acc_s)
    zng = zng_ref[0]
    for r in range(GRP):
        o_win = acc_s[r] / l_s[r]
        g_cmp = _sigmoid(zng[r:r + 1, :])
        g_slc = _sigmoid(zng[GRP + r:GRP + r + 1, :])
        g_win = _sigmoid(zng[2 * GRP + r:2 * GRP + r + 1, :])
        o_ref[r] = (g_cmp * oc_ref[r] + g_slc * os_ref[r] + g_win * o_win).astype(o_ref.dtype)


def _tiles_rows(a, tq):
    g, t, c = a.shape
    return a.reshape(g, t // tq, tq, c)


def _tiles_cols(a, tq):
    g, c, t = a.shape
    return a.reshape(g, c, t // tq, tq).transpose(0, 2, 1, 3)


def nsa_prompt(z, pe, w1, b1, w2, tq=256):
    t = z.shape[0]
    nblk = t // D_CMP
    n_slc = t // BLK_SLC
    k_sel = min(TOP_N, n_slc)
    nq = t // tq
    zq = z[:, OFF_Q:OFF_Q + N_HEADS * HEAD_DIM]
    kv6 = z[:, OFF_KV:OFF_KV + N_KV * 6 * HEAD_DIM].reshape(t, N_KV, 6, HEAD_DIM)
    zng = z[:, OFF_NG:OFF_NG + 3 * N_HEADS]
    qT = (zq.reshape(t, N_HEADS, HEAD_DIM).transpose(1, 2, 0) * (HEAD_DIM ** -0.5)).astype(BF16)
    x = kv6[:, :, 0:2].reshape(nblk, D_CMP, N_KV, 2, HEAD_DIM).transpose(2, 3, 0, 1, 4)
    cmp = nsa_compress_blocks(x.reshape(2 * N_KV, nblk, D_CMP * HEAD_DIM), pe, w1, b1, w2)
    kc = cmp[0::2].astype(BF16)
    vcT = cmp[1::2].transpose(0, 2, 1).astype(BF16)
    i0 = np.arange(nblk)[None, :] * D_CMP
    j0 = np.arange(n_slc)[:, None] * BLK_SLC
    ovT = jnp.asarray(((i0 < j0 + BLK_SLC) & (i0 + L_CMP > j0)).astype(np.float32), dtype=BF16)

    head_blk = pl.BlockSpec((GRP, HEAD_DIM, tq), lambda g, i: (g, 0, i))
    scratch = [pltpu.VMEM((GRP, 1, tq), F32), pltpu.VMEM((GRP, 1, tq), F32), pltpu.VMEM((GRP, HEAD_DIM, tq), F32)]
    o_cmp, bias = pl.pallas_call(
        functools.partial(_cmp_kernel, tq=tq, k_sel=k_sel),
        grid=(N_KV, nq),
        in_specs=[head_blk,
                  pl.BlockSpec((1, nblk, HEAD_DIM), lambda g, i: (g, 0, 0)),
                  pl.BlockSpec((1, HEAD_DIM, nblk), lambda g, i: (g, 0, 0)),
                  pl.BlockSpec((n_slc, nblk), lambda g, i: (0, 0))],
        out_specs=[head_blk, pl.BlockSpec((1, n_slc, tq), lambda g, i: (g, 0, i))],
        out_shape=[jax.ShapeDtypeStruct((N_HEADS, HEAD_DIM, t), F32),
                   jax.ShapeDtypeStruct((N_KV, n_slc, t), BF16)],
        compiler_params=_params(("parallel", "arbitrary")),
        name="nsa_cmp_select",
    )(qT, kc, vcT, ovT)

    k_slc = kv6[:, :, 2].transpose(1, 0, 2)
    onehot = (jnp.arange(t)[:, None] // BLK_SLC == jnp.arange(n_slc)[None, :]).astype(F32)
    k_aug = jnp.concatenate([k_slc, jnp.broadcast_to(onehot, (N_KV, t, n_slc))], axis=-1).astype(BF16)
    v_slcT = kv6[:, :, 3].transpose(1, 2, 0).astype(BF16)
    ka = HEAD_DIM + n_slc
    o_slc = pl.pallas_call(
        functools.partial(_slc_kernel, tq=tq),
        grid=(N_KV, nq),
        in_specs=[head_blk,
                  pl.BlockSpec((1, n_slc, tq), lambda g, i: (g, 0, i)),
                  pl.BlockSpec((1, nq, tq, ka), lambda g, i: (g, 0, 0, 0)),
                  pl.BlockSpec((1, nq, HEAD_DIM, tq), lambda g, i: (g, 0, 0, 0))],
        out_specs=head_blk,
        out_shape=jax.ShapeDtypeStruct((N_HEADS, HEAD_DIM, t), F32),
        scratch_shapes=[pltpu.VMEM((ka, GRP * tq), BF16), pltpu.VMEM((1, GRP * tq), F32),
                        pltpu.VMEM((1, GRP * tq), F32), pltpu.VMEM((HEAD_DIM, GRP * tq), F32)],
        compiler_params=_params(("parallel", "arbitrary")),
        name="nsa_selected",
    )(qT, bias, _tiles_rows(k_aug, tq), _tiles_cols(v_slcT, tq))

    k_win = kv6[:, :, 4].transpose(1, 0, 2).astype(BF16)
    v_winT = kv6[:, :, 5].transpose(1, 2, 0).astype(BF16)
    zngT = zng.reshape(t, N_KV, GRP, 3).transpose(1, 3, 2, 0).reshape(N_KV, 3 * GRP, t)
    oT = pl.pallas_call(
        functools.partial(_win_kernel, tq=tq),
        grid=(N_KV, nq),
        in_specs=[head_blk,
                  pl.BlockSpec((1, nq, tq, HEAD_DIM), lambda g, i: (g, 0, 0, 0)),
                  pl.BlockSpec((1, nq, HEAD_DIM, tq), lambda g, i: (g, 0, 0, 0)),
                  head_blk, head_blk,
                  pl.BlockSpec((1, 3 * GRP, tq), lambda g, i: (g, 0, i))],
        out_specs=head_blk,
        out_shape=jax.ShapeDtypeStruct((N_HEADS, HEAD_DIM, t), BF16),
        scratch_shapes=scratch,
        compiler_params=_params(("parallel", "arbitrary")),
        name="nsa_window_combine",
    )(qT, _tiles_rows(k_win, tq), _tiles_cols(v_winT, tq), o_cmp, o_slc, zngT)
    o = oT.transpose(2, 0, 1).reshape(t, N_HEADS * HEAD_DIM)
    return o, kv6


def _nt_dot(a, b):
    return lax.dot_general(a.astype(BF16), b.astype(BF16), (((1,), (1,)), ((), ())), preferred_element_type=F32)


def _nsa_sample_kernel(pt_ref, q_ref, zng_ref, kns_ref, knw_ref, win_ref, *rest, n_pages, past, t_new):
    cmp_pages = rest[:n_pages]
    slc_pages = rest[n_pages:2 * n_pages]
    pa_ref, pb_ref, wa_ref, wb_ref, b1_ref, w2_ref, ov_ref, ex_ref, o_ref = rest[2 * n_pages:]
    del pt_ref
    rows = GRP * t_new
    nblk = n_pages * (PAGE_SIZE // D_CMP)
    n_cmp = nblk - 1
    n_slc = -(-(past + t_new) // BLK_SLC)
    k_sel = min(TOP_N, n_slc)
    wlen = win_ref.shape[1]
    lanes = 128
    lane = lax.broadcasted_iota(jnp.int32, (rows, lanes), 1)
    tok = lax.broadcasted_iota(jnp.int32, (rows, lanes), 0) % t_new
    qpos = past + tok

    cmp = []
    for s in range(2):
        x = jnp.concatenate([cmp_pages[p][0, g * 2 + s] for g in range(N_KV) for p in range(n_pages)], axis=0)
        ha = _bdot(x + pa_ref[s], wa_ref[s])
        hb = _bdot(x + pb_ref[s], wb_ref[s])
        h = _silu(ha + pltpu.roll(hb, N_KV * nblk - 1, 0) + b1_ref[s])
        cmp.append(_bdot(h, w2_ref[s]))

    new_ok = (lane < t_new) & (lane <= tok)
    zeros_tail = jnp.zeros((lanes - kns_ref.shape[2], lanes), BF16)
    for g in range(N_KV):
        q = q_ref[0, g]
        kc = cmp[0][g * nblk:(g + 1) * nblk]
        vc = cmp[1][g * nblk:(g + 1) * nblk]
        vis = (lane * D_CMP + (L_CMP - 1) <= qpos) & (lane < n_cmp)
        s = jnp.where(vis, _nt_dot(q[:, :HEAD_DIM], kc), NEG)
        e = jnp.exp(s - jnp.max(s, axis=1, keepdims=True))
        p = e / jnp.sum(e, axis=1, keepdims=True) * vis.astype(F32)
        o_cmp = _bdot(p, vc)
        psum = p
        for r in range(1, GRP):
            psum = psum + pltpu.roll(p, r * t_new, 0)
        p_hi, p_lo = _split_bf16(psum)
        ov = ov_ref[...]
        imp = jnp.dot(p_hi, ov, preferred_element_type=F32) + jnp.dot(p_lo, ov, preferred_element_type=F32)
        cur = qpos // BLK_SLC
        valid = (lane <= cur) & (lane < n_slc)
        forced = (lane == 0) | (lane == cur) | (lane == cur - 1)
        score = jnp.where(lane < n_slc, jnp.where(forced, BIG, jnp.where(valid, imp, -BIG)), -3e38)
        rank = jnp.zeros((rows, lanes), F32)
        for sh in range(1, n_slc):
            lower = pltpu.roll(score, sh, 1)
            upper = pltpu.roll(score, lanes - sh, 1)
            rank = rank + (lower >= score).astype(F32) + (upper > score).astype(F32)
        sel = (rank < k_sel) & valid
        bias = _bdot(jnp.where(sel, 0.0, NEG), ex_ref[...])
        kv_new = jnp.concatenate([kns_ref[0, g], zeros_tail], axis=0)
        sc = [_nt_dot(q, slc_pages[p][0, g]) + bias[:, p * lanes:(p + 1) * lanes] for p in range(n_pages)]
        sc.append(jnp.where(new_ok, _nt_dot(q, kv_new) + bias[:, n_pages * lanes:(n_pages + 1) * lanes], NEG))
        m = sc[0].max(axis=1, keepdims=True)
        for x in sc[1:]:
            m = jnp.maximum(m, x.max(axis=1, keepdims=True))
        den = jnp.zeros((rows, 1), F32)
        acc = jnp.zeros((rows, lanes), F32)
        for p in range(n_pages + 1):
            e = jnp.exp(sc[p] - m)
            den = den + jnp.sum(e, axis=1, keepdims=True)
            acc = acc + _bdot(e, slc_pages[p][0, g] if p < n_pages else kv_new)
        o_slc = (acc / den)[:, HEAD_DIM:]
        kw_new = jnp.concatenate([knw_ref[0, g], zeros_tail], axis=0)
        kvw = [win_ref[0, c * lanes:(c + 1) * lanes, g * lanes:(g + 1) * lanes] for c in range(wlen // lanes)]
        sc = []
        for c in range(wlen // lanes):
            wpos = past - wlen + c * lanes + lane
            ok = (wpos <= qpos) & (wpos > qpos - WINDOW) & (wpos >= 0)
            sc.append(jnp.where(ok, _nt_dot(q, kvw[c]), NEG))
        kvw.append(kw_new)
        sc.append(jnp.where(new_ok, _nt_dot(q, kw_new), NEG))
        m = sc[0].max(axis=1, keepdims=True)
        for x in sc[1:]:
            m = jnp.maximum(m, x.max(axis=1, keepdims=True))
        den = jnp.zeros((rows, 1), F32)
        acc = jnp.zeros((rows, lanes), F32)
        for c in range(len(sc)):
            e = jnp.exp(sc[c] - m)
            den = den + jnp.sum(e, axis=1, keepdims=True)
            acc = acc + _bdot(e, kvw[c])
        o_win = (acc / den)[:, HEAD_DIM:]
        gates = _sigmoid(zng_ref[0, g])
        o_ref[0, g] = gates[:, 0:1] * o_cmp + gates[:, 1:2] * o_slc + gates[:, 2:3] * o_win


def nsa_sample(z, pool_l, page_table, win_buf, pe, w1, b1, w2):
    bsz, n_pages = page_table.shape
    t_new = z.shape[0] // bsz
    past = n_pages * PAGE_SIZE
    wlen = win_buf.shape[1]
    n_pool = pool_l.shape[0]
    rows = GRP * t_new
    nblk = past // D_CMP
    n_slc = -(-(past + t_new) // BLK_SLC)
    lanes = 128
    assert nblk == lanes and n_slc <= lanes and wlen % lanes == 0 and t_new <= 8

    def per_group(a, width):
        return a.reshape(bsz, t_new, N_KV, GRP, width).transpose(0, 2, 3, 1, 4).reshape(bsz, N_KV, rows, width)

    zq = per_group(z[:, OFF_Q:OFF_Q + N_HEADS * HEAD_DIM], HEAD_DIM) * (HEAD_DIM ** -0.5)
    q_pad = jnp.pad(zq, ((0, 0), (0, 0), (0, 0), (0, lanes - HEAD_DIM))).astype(BF16)
    zng = jnp.pad(per_group(z[:, OFF_NG:OFF_NG + 3 * N_HEADS], 3), ((0, 0), (0, 0), (0, 0), (0, lanes - 3)))
    kv6 = z[:, OFF_KV:OFF_KV + N_KV * 6 * HEAD_DIM].reshape(bsz, t_new, N_KV, 6, HEAD_DIM)

    def new_rows(c0):
        a = kv6[:, :, :, c0:c0 + 2].transpose(0, 2, 1, 3, 4).reshape(bsz, N_KV, t_new, 2 * HEAD_DIM)
        return jnp.pad(a, ((0, 0), (0, 0), (0, 8 - t_new), (0, 0))).astype(BF16)

    per_page = PAGE_SIZE // D_CMP
    pool_cmp = (pool_l[:, :, :, 0:2].reshape(n_pool, per_page, D_CMP, N_KV, 2, HEAD_DIM)
                .transpose(0, 3, 4, 1, 2, 5).reshape(n_pool, 2 * N_KV, per_page, D_CMP * HEAD_DIM))
    pool_slc = pool_l[:, :, :, 2:4].transpose(0, 2, 1, 3, 4).reshape(n_pool, N_KV, PAGE_SIZE, 2 * HEAD_DIM).astype(BF16)
    win = win_buf.reshape(bsz, wlen, N_KV * 2 * HEAD_DIM).astype(BF16)

    kd = D_CMP * HEAD_DIM
    pa = pe[:, :D_CMP].reshape(2, 1, kd)
    pb = pe[:, D_CMP:].reshape(2, 1, kd)
    wa = w1[:, :D_CMP].reshape(2, kd, CMP_HID).astype(BF16)
    wb = w1[:, D_CMP:].reshape(2, kd, CMP_HID).astype(BF16)
    i0 = np.arange(lanes)[:, None] * D_CMP
    j0 = np.arange(lanes)[None, :] * BLK_SLC
    ov = ((i0 < j0 + BLK_SLC) & (i0 + L_CMP > j0) & (np.arange(lanes)[:, None] < nblk - 1)
          & (np.arange(lanes)[None, :] < n_slc))
    ov = jnp.asarray(ov.astype(np.float32), dtype=BF16)
    nkeys = (n_pages + 1) * lanes
    ex = jnp.asarray((np.arange(nkeys)[None, :] // BLK_SLC == np.arange(lanes)[:, None]).astype(np.float32), dtype=BF16)

    full = lambda shape: pl.BlockSpec(shape, lambda b, pt: (0,) * len(shape))
    per_b = lambda shape: pl.BlockSpec((1,) + shape, lambda b, pt: (b,) + (0,) * len(shape))
    page = lambda shape, p: pl.BlockSpec((1,) + shape, lambda b, pt, p=p: (pt[b, p],) + (0,) * len(shape))
    in_specs = ([per_b((N_KV, rows, lanes)), per_b((N_KV, rows, lanes)), per_b((N_KV, 8, lanes)),
                 per_b((N_KV, 8, lanes)), per_b((wlen, N_KV * lanes))]
                + [page((2 * N_KV, per_page, kd), p) for p in range(n_pages)]
                + [page((N_KV, PAGE_SIZE, lanes), p) for p in range(n_pages)]
                + [full((2, 1, kd)), full((2, 1, kd)), full((2, kd, CMP_HID)), full((2, kd, CMP_HID)),
                   full((2, 1, CMP_HID)), full((2, CMP_HID, HEAD_DIM)), full((lanes, lanes)), full((lanes, nkeys))])
    o = pl.pallas_call(
        functools.partial(_nsa_sample_kernel, n_pages=n_pages, past=past, t_new=t_new),
        grid_spec=pltpu.PrefetchScalarGridSpec(
            num_scalar_prefetch=1, grid=(bsz,), in_specs=in_specs,
            out_specs=pl.BlockSpec((1, N_KV, rows, HEAD_DIM), lambda b, pt: (b, 0, 0, 0))),
        out_shape=jax.ShapeDtypeStruct((bsz, N_KV, rows, HEAD_DIM), F32),
        compiler_params=_params(("arbitrary",)),
        name="nsa_sample",
    )(page_table, q_pad, zng, new_rows(2), new_rows(4), win,
      *([pool_cmp] * n_pages), *([pool_slc] * n_pages),
      pa, pb, wa, wb, b1.reshape(2, 1, CMP_HID), w2.astype(BF16), ov, ex)
    o = o.reshape(bsz, N_KV, GRP, t_new, HEAD_DIM).transpose(0, 3, 1, 2, 4).reshape(bsz * t_new, N_HEADS * HEAD_DIM)
    return o, kv6


HG_SUB = 16
HG_BLOCK = 256


def _lower_bound(lbl, layer):
    e = jnp.exp(lbl - jnp.max(lbl, axis=0, keepdims=True))
    p = e / jnp.sum(e, axis=0, keepdims=True)
    lb = jnp.zeros((1, lbl.shape[1]), F32)
    for i in range(1, layer + 1):
        lb = lb + p[i:i + 1, :]
    return lb


def _hgrn_kernel(zq_ref, zf_ref, zi_ref, zg_ref, lbl_ref, ng_ref, ltri_ref, o_ref, s_ref, st_s, *, layer):
    i = pl.program_id(1)
    tb = zq_ref.shape[0]
    c = HG_SUB
    nc = tb // c

    @pl.when(i == 0)
    def _():
        st_s[...] = jnp.zeros(st_s.shape, F32)

    lb = _lower_bound(lbl_ref[...], layer)
    q = _silu(zq_ref[...])
    f = lb + (1.0 - lb) * _sigmoid(zf_ref[...])
    k = 1.0 - f
    v = zi_ref[...]
    logf = jnp.log(f)
    h1 = logf.astype(BF16)
    r1 = logf - h1.astype(F32)
    h2 = r1.astype(BF16)
    h3 = (r1 - h2.astype(F32)).astype(BF16)
    ltri = ltri_ref[...]
    g = (jnp.dot(ltri, h1, preferred_element_type=F32) + jnp.dot(ltri, h2, preferred_element_type=F32)
         + jnp.dot(ltri, h3, preferred_element_type=F32))
    g3 = g.reshape(nc, c, DK_HG)
    q3 = q.reshape(nc, c, DK_HG)
    k3 = k.reshape(nc, c, DK_HG)
    v3 = v.reshape(nc, c, DV_HG)
    g_last = g3[:, c - 1:c, :]
    qe = (q * jnp.exp(g)).astype(BF16)
    kd = (k3 * jnp.exp(g_last - g3)).reshape(tb, DK_HG).astype(BF16)
    dec = jnp.exp(g_last)

    tpos = lax.broadcasted_iota(jnp.int32, (nc, c, DK_HG), 1)
    o_intra = jnp.zeros((nc, c, DV_HG), F32)
    for s in range(c):
        d = jnp.exp(jnp.where(tpos >= s, g3 - g3[:, s:s + 1, :], 0.0))
        w = jnp.where(tpos >= s, q3 * k3[:, s:s + 1, :] * d, 0.0)
        o_intra = o_intra + jnp.sum(w, axis=-1, keepdims=True) * v3[:, s:s + 1, :]

    vT = v.T
    lane_chunk = lax.broadcasted_iota(jnp.int32, (DV_HG, tb), 1) // c
    st = st_s[...]
    o_inter = []
    for ci in range(nc):
        o_inter.append(lax.dot_general(qe[ci * c:(ci + 1) * c, :], st.astype(BF16), (((1,), (1,)), ((), ())),
                                       preferred_element_type=F32))
        ut = jnp.dot(jnp.where(lane_chunk == ci, vT, 0.0).astype(BF16), kd, preferred_element_type=F32)
        st = st * dec[ci] + ut
    st_s[...] = st
    o = jnp.concatenate(o_inter, axis=0) + o_intra.reshape(tb, DV_HG)
    o = o * lax.rsqrt(jnp.mean(o * o, axis=-1, keepdims=True) + LN_EPS) * ng_ref[...]
    o_ref[...] = (o * _silu(zg_ref[...])).astype(o_ref.dtype)

    @pl.when(i == pl.num_programs(1) - 1)
    def _():
        s_ref[0] = st.T


def hgrn_prompt(z, lb_logits, norm_g, layer):
    t = z.shape[0]
    tb = min(HG_BLOCK, t)
    r = np.arange(tb)
    ltri = jnp.asarray(((r[:, None] // HG_SUB == r[None, :] // HG_SUB) & (r[None, :] <= r[:, None])).astype(np.float32),
                       dtype=BF16)
    col = lambda cidx: pl.BlockSpec((tb, DK_HG), lambda h, i, o=(OFF_HG + cidx * H_HG * DK_HG) // DK_HG: (i, o + h))
    return pl.pallas_call(
        functools.partial(_hgrn_kernel, layer=layer),
        grid=(H_HG, t // tb),
        in_specs=[col(0), col(1), col(2), col(3),
                  pl.BlockSpec((DEPTH, DK_HG), lambda h, i: (0, h)),
                  pl.BlockSpec((1, DV_HG), lambda h, i: (0, h)),
                  pl.BlockSpec((tb, tb), lambda h, i: (0, 0))],
        out_specs=[pl.BlockSpec((tb, DV_HG), lambda h, i: (i, h)),
                   pl.BlockSpec((1, DK_HG, DV_HG), lambda h, i: (h, 0, 0))],
        out_shape=[jax.ShapeDtypeStruct((t, H_HG * DV_HG), BF16),
                   jax.ShapeDtypeStruct((H_HG, DK_HG, DV_HG), F32)],
        scratch_shapes=[pltpu.VMEM((DV_HG, DK_HG), F32)],
        compiler_params=_params(("parallel", "arbitrary")),
        name="hgrn_prompt",
    )(z, z, z, z, lb_logits, norm_g, ltri)


def _hgrn_step_kernel(zq_ref, zf_ref, zi_ref, zg_ref, s0_ref, lbl_ref, ng_ref, o_ref, s_ref, *, layer):
    t_new = zq_ref.shape[1]
    lb = _lower_bound(lbl_ref[...], layer)
    q_all = _silu(zq_ref[0])
    f_all = lb + (1.0 - lb) * _sigmoid(zf_ref[0])
    v_all = zi_ref[0]
    zg_all = zg_ref[0]
    ng = ng_ref[...]
    tpos = lax.broadcasted_iota(jnp.int32, (t_new, DK_HG), 0)
    pad_rows = jnp.zeros((8 - t_new, DK_HG), F32)
    outs = []
    for h in range(H_HG):
        sl = slice(h * DK_HG, (h + 1) * DK_HG)
        q, f, v = q_all[:, sl], f_all[:, sl], v_all[:, sl]
        k = 1.0 - f
        logf = jnp.log(f)
        rows = [logf[0:1, :]]
        for t in range(1, t_new):
            rows.append(rows[-1] + logf[t:t + 1, :])
        g = jnp.concatenate(rows, axis=0)
        g_last = rows[-1]
        s0 = s0_ref[0, h]
        qe = jnp.concatenate([q * jnp.exp(g), pad_rows], axis=0)
        o = _bdot(qe, s0)[0:t_new]
        for s in range(t_new):
            d = jnp.exp(jnp.where(tpos >= s, g - g[s:s + 1, :], 0.0))
            w = jnp.where(tpos >= s, q * k[s:s + 1, :] * d, 0.0)
            o = o + jnp.sum(w, axis=-1, keepdims=True) * v[s:s + 1, :]
        kd = k * jnp.exp(g_last - g)
        cols = jnp.concatenate([jnp.exp(g_last), kd, jnp.zeros((DK_HG - 1 - t_new, DK_HG), F32)], axis=0).T
        s_new = s0 * cols[:, 0:1]
        for s in range(t_new):
            s_new = s_new + cols[:, 1 + s:2 + s] * v[s:s + 1, :]
        s_ref[0, h] = s_new
        o = o * lax.rsqrt(jnp.mean(o * o, axis=-1, keepdims=True) + LN_EPS) * ng[:, sl]
        outs.append(o * _silu(zg_all[:, sl]))
    o_ref[0] = jnp.concatenate(outs, axis=1).astype(o_ref.dtype)


def hgrn_step(z3, s0, lb_logits, norm_g, layer):
    bsz, t_new = z3.shape[:2]
    w = H_HG * DK_HG
    col = lambda cidx: pl.BlockSpec((1, t_new, w), lambda b, o=(OFF_HG + cidx * w) // w: (b, 0, o))
    assert OFF_HG % w == 0
    st = pl.BlockSpec((1, H_HG, DK_HG, DV_HG), lambda b: (b, 0, 0, 0))
    return pl.pallas_call(
        functools.partial(_hgrn_step_kernel, layer=layer),
        grid=(bsz,),
        in_specs=[col(0), col(1), col(2), col(3), st,
                  pl.BlockSpec((DEPTH, w), lambda b: (0, 0)), pl.BlockSpec((1, w), lambda b: (0, 0))],
        out_specs=[pl.BlockSpec((1, t_new, w), lambda b: (b, 0, 0)), st],
        out_shape=[jax.ShapeDtypeStruct((bsz, t_new, w), BF16), jax.ShapeDtypeStruct(s0.shape, F32)],
        compiler_params=_params(("parallel",)),
        name="hgrn_step",
    )(z3, z3, z3, z3, s0, lb_logits, norm_g)


CONV_HALO = 32


def _glu_ln_silu(y, g_ref, b_ref):
    mu = jnp.mean(y, axis=-1, keepdims=True)
    d = y - mu
    var = jnp.mean(d * d, axis=-1, keepdims=True)
    return _silu(d * lax.rsqrt(var + LN_EPS) * g_ref[...] + b_ref[...])


def _conv_kernel(a_ref, g_ref, w_ref, b_ref, lg_ref, lb_ref, o_ref, tail_ref, buf_s):
    i = pl.program_id(0)
    tb = a_ref.shape[0]

    @pl.when(i == 0)
    def _():
        buf_s[0:CONV_HALO, :] = jnp.zeros((CONV_HALO, D_CONV), F32)

    buf_s[CONV_HALO:, :] = a_ref[...] * _sigmoid(g_ref[...])
    off = CONV_HALO - (CONV_W - 1)
    y = jnp.zeros((tb, D_CONV), F32) + b_ref[...]
    for k in range(CONV_W):
        y = y + buf_s[off + k:off + k + tb, :] * w_ref[k:k + 1, :]
    o_ref[...] = _glu_ln_silu(y, lg_ref, lb_ref).astype(o_ref.dtype)
    tail = buf_s[tb:tb + CONV_HALO, :]
    tail_ref[...] = tail
    buf_s[0:CONV_HALO, :] = tail


def conv_prompt(z, w_dw, b_dw, ln_g, ln_b, tb=512):
    t = z.shape[0]
    tb = min(tb, t)
    vec = pl.BlockSpec((1, D_CONV), lambda i: (0, 0))
    o, tail = pl.pallas_call(
        _conv_kernel,
        grid=(t // tb,),
        in_specs=[pl.BlockSpec((tb, D_CONV), lambda i: (i, OFF_CONV // D_CONV)),
                  pl.BlockSpec((tb, D_CONV), lambda i: (i, OFF_CONV // D_CONV + 1)),
                  pl.BlockSpec((CONV_W, D_CONV), lambda i: (0, 0)), vec, vec, vec],
        out_specs=[pl.BlockSpec((tb, D_CONV), lambda i: (i, 0)), pl.BlockSpec((CONV_HALO, D_CONV), lambda i: (0, 0))],
        out_shape=[jax.ShapeDtypeStruct((t, D_CONV), BF16), jax.ShapeDtypeStruct((CONV_HALO, D_CONV), F32)],
        scratch_shapes=[pltpu.VMEM((tb + CONV_HALO, D_CONV), F32)],
        compiler_params=_params(("arbitrary",)),
        name="conv_prompt",
    )(z, z, w_dw, b_dw[None], ln_g[None], ln_b[None])
    return o, tail[CONV_HALO - (CONV_W - 1):]


def _conv_step_kernel(a_ref, g_ref, st_ref, wsh_ref, wnew_ref, b_ref, lg_ref, lb_ref, o_ref, ns_ref):
    t_new = a_ref.shape[1]
    hist = CONV_W - 1
    u = [a_ref[:, s, :] * _sigmoid(g_ref[:, s, :]) for s in range(t_new)]
    st = st_ref[...]
    for t in range(t_new):
        y = jnp.sum(st * wsh_ref[t], axis=1) + b_ref[...]
        for s in range(t + 1):
            y = y + u[s] * wnew_ref[t, s:s + 1, :]
        o_ref[:, t, :] = _glu_ln_silu(y, lg_ref, lb_ref).astype(o_ref.dtype)
    ns_ref[:, 0:hist - t_new, :] = st_ref[:, t_new:, :]
    for s in range(t_new):
        ns_ref[:, hist - t_new + s, :] = u[s]


def conv_step(z3, state, w_dw, b_dw, ln_g, ln_b, bb=8):
    bsz, t_new = z3.shape[:2]
    hist = CONV_W - 1
    wsh = jnp.stack([jnp.concatenate([jnp.zeros((t, D_CONV), F32), w_dw[:hist - t]], axis=0) for t in range(t_new)])
    wnew = jnp.stack([jnp.concatenate([w_dw[hist - t:], jnp.zeros((t_new - 1 - t, D_CONV), F32)], axis=0)
                      for t in range(t_new)])
    vec = pl.BlockSpec((1, D_CONV), lambda i: (0, 0))
    return pl.pallas_call(
        _conv_step_kernel,
        grid=(bsz // bb,),
        in_specs=[pl.BlockSpec((bb, t_new, D_CONV), lambda i: (i, 0, OFF_CONV // D_CONV)),
                  pl.BlockSpec((bb, t_new, D_CONV), lambda i: (i, 0, OFF_CONV // D_CONV + 1)),
                  pl.BlockSpec((bb, hist, D_CONV), lambda i: (i, 0, 0)),
                  pl.BlockSpec((t_new, hist, D_CONV), lambda i: (0, 0, 0)),
                  pl.BlockSpec((t_new, t_new, D_CONV), lambda i: (0, 0, 0)), vec, vec, vec],
        out_specs=[pl.BlockSpec((bb, t_new, D_CONV), lambda i: (i, 0, 0)),
                   pl.BlockSpec((bb, hist, D_CONV), lambda i: (i, 0, 0))],
        out_shape=[jax.ShapeDtypeStruct((bsz, t_new, D_CONV), BF16), jax.ShapeDtypeStruct(state.shape, F32)],
        compiler_params=_params(("parallel",)),
        name="conv_step",
    )(z3, z3, state, wsh, wnew, b_dw[None], ln_g[None], ln_b[None])


def _layer_norm(x, g, b):
    mu = jnp.mean(x, -1, keepdims=True)
    var = jnp.mean(jnp.square(x - mu), -1, keepdims=True)
    return (x - mu) * lax.rsqrt(var + LN_EPS) * g + b


def _conv_pre(z_conv, prev, w_dw, b_dw, ln_g, ln_b):
    a, g = jnp.split(z_conv, 2, axis=-1)
    u = a * jax.nn.sigmoid(g)
    padded = jnp.concatenate([prev, u], axis=1)
    y = lax.conv_general_dilated(padded, w_dw[:, None, :], window_strides=(1,), padding='VALID',
                                 dimension_numbers=('NWC', 'WIO', 'NWC'), feature_group_count=D_CONV) + b_dw
    return jax.nn.silu(_layer_norm(y, ln_g, ln_b)), padded[:, -(CONV_W - 1):]


def _gated_recurrence(q, k, v, logf, s0):
    B, T, H, DK = q.shape
    C = min(HG_CHUNK, T)
    n = T // C

    def blk(a):
        return jnp.moveaxis(a.reshape(B, n, C, *a.shape[2:]), 1, 0)

    tri = jnp.tril(jnp.ones((C, C), bool))[None, :, :, None, None]

    def step(S, xs):
        qc, kc, vc, gc = xs
        G = jnp.cumsum(gc, axis=1)
        o_inter = jnp.einsum('bchk,bhkv->bchv', qc * jnp.exp(G), S)
        decay = jnp.exp(jnp.where(tri, G[:, :, None] - G[:, None, :], -jnp.inf))
        A = jnp.einsum('bthk,bshk,btshk->bths', qc, kc, decay)
        o_intra = jnp.einsum('bths,bshv->bthv', A, vc)
        G_last = G[:, -1]
        S_new = jnp.exp(G_last)[..., None] * S + jnp.einsum('bshk,bshv->bhkv', kc * jnp.exp(G_last[:, None] - G), vc)
        return S_new, o_inter + o_intra

    s_fin, o = lax.scan(step, s0, (blk(q), blk(k), blk(v), blk(logf)))
    return jnp.moveaxis(o, 0, 1).reshape(B, T, H, v.shape[-1]), s_fin


def _hgrn_pre(zh, s0, lb, norm_g):
    B, T = zh.shape[:2]
    zq, zf, zi, zg = jnp.split(zh, 4, axis=-1)
    q = jax.nn.silu(zq).reshape(B, T, H_HG, DK_HG)
    lbh = lb.reshape(H_HG, DK_HG)
    f = lbh + (1.0 - lbh) * jax.nn.sigmoid(zf).reshape(B, T, H_HG, DK_HG)
    v = zi.reshape(B, T, H_HG, DV_HG)
    o, s_new = _gated_recurrence(q, 1.0 - f, v, jnp.log(f), s0)
    o = o * lax.rsqrt(jnp.mean(jnp.square(o), -1, keepdims=True) + LN_EPS) * norm_g.reshape(H_HG, DV_HG)
    o = o * jax.nn.silu(zg).reshape(B, T, H_HG, DV_HG)
    return o.reshape(B, T, H_HG * DV_HG), s_new


def _nsa_compress_jax(kv, pe, w1, b1, w2):
    B, L = kv.shape[:2]
    n16 = L // D_CMP
    x = kv[:, :n16 * D_CMP].reshape(B, n16, D_CMP, N_KV, 2, HEAD_DIM)
    pe_t = jnp.transpose(pe, (1, 0, 2))
    ha = jnp.einsum('bnjgsd,sjdh->bngsh', x + pe_t[None, None, :D_CMP, None], w1[:, :D_CMP])
    hb = jnp.einsum('bnjgsd,sjdh->bngsh', x + pe_t[None, None, D_CMP:, None], w1[:, D_CMP:])
    h = jax.nn.silu(ha[:, :-1] + hb[:, 1:] + b1)
    return jnp.einsum('bngsh,shd->bngsd', h, w2)


def _nsa_attend_jax(q, gates, qpos, cmp_kv, slc_kv, win_kv, wpos):
    scale = HEAD_DIM ** -0.5
    B, Tq = q.shape[:2]
    L = slc_kv.shape[1]
    n_cmp = cmp_kv.shape[1]
    cmp_last = jnp.arange(n_cmp) * D_CMP + L_CMP - 1
    vis = (cmp_last[None, :] <= qpos[:, None])[None, :, None, None, :]
    s = jnp.einsum('btgrd,bigd->btgri', q, cmp_kv[..., 0, :]) * scale
    p_cmp = jax.nn.softmax(jnp.where(vis, s, NEG), axis=-1) * vis
    o_cmp = jnp.einsum('btgri,bigd->btgrd', p_cmp, cmp_kv[..., 1, :])
    n_slc = -(-L // BLK_SLC)
    i_start = jnp.arange(n_cmp) * D_CMP
    j_start = jnp.arange(n_slc) * BLK_SLC
    overlap = ((i_start[:, None] < j_start[None] + BLK_SLC) & (i_start[:, None] + L_CMP > j_start[None])).astype(F32)
    imp = jnp.einsum('btgri,ij->btgj', p_cmp, overlap)
    cur = qpos // BLK_SLC
    jj = jnp.arange(n_slc)[None]
    valid = (jj <= cur[:, None])[None, :, None]
    forced = ((jj == 0) | (jj == cur[:, None]) | (jj == cur[:, None] - 1))[None, :, None]
    score = jnp.where(forced, BIG, jnp.where(valid, imp, -BIG))
    k_sel = min(TOP_N, n_slc)
    _, idx = lax.top_k(score, k_sel)
    pos = (idx[..., None] * BLK_SLC + jnp.arange(BLK_SLC)).reshape(B, Tq, N_KV, k_sel * BLK_SLC)
    ok = (pos <= qpos[None, :, None, None])[:, :, :, None, :]
    bi = jnp.arange(B)[:, None, None, None]
    gi = jnp.arange(N_KV)[None, None, :, None]
    kv_sel = slc_kv[bi, jnp.minimum(pos, L - 1), gi]
    s = jnp.einsum('btgrd,btgkd->btgrk', q, kv_sel[..., 0, :]) * scale
    p = jax.nn.softmax(jnp.where(ok, s, NEG), axis=-1)
    o_slc = jnp.einsum('btgrk,btgkd->btgrd', p, kv_sel[..., 1, :])
    wok = ((wpos[None] <= qpos[:, None]) & (wpos[None] > qpos[:, None] - WINDOW) & (wpos[None] >= 0))[None, :, None, None, :]
    s = jnp.einsum('btgrd,bsgd->btgrs', q, win_kv[..., 0, :]) * scale
    p = jax.nn.softmax(jnp.where(wok, s, NEG), axis=-1)
    o_win = jnp.einsum('btgrs,bsgd->btgrd', p, win_kv[..., 1, :])
    return gates[..., 0:1] * o_cmp + gates[..., 1:2] * o_slc + gates[..., 2:3] * o_win


def _nsa_sample_jax(zq, zng, kv6, pool_l, page_table, win_buf, pe, w1, b1, w2):
    B, T = zq.shape[:2]
    past = page_table.shape[1] * PAGE_SIZE
    gates = jax.nn.sigmoid(zng)
    past_cmp = pool_l[page_table, :, :, 0:2].reshape(B, past, N_KV, 2, HEAD_DIM)
    past_slc = pool_l[page_table, :, :, 2:4].reshape(B, past, N_KV, 2, HEAD_DIM)
    full_cmp = jnp.concatenate([past_cmp, kv6[..., 0:2, :]], axis=1)
    full_slc = jnp.concatenate([past_slc, kv6[..., 2:4, :]], axis=1)
    cmp_kv = _nsa_compress_jax(full_cmp, pe, w1, b1, w2)
    wlen = win_buf.shape[1]
    wkv = jnp.concatenate([win_buf, kv6[..., 4:6, :]], axis=1)
    wpos = past - wlen + jnp.arange(wlen + T, dtype=jnp.int32)
    qpos = past + jnp.arange(T, dtype=jnp.int32)
    o = _nsa_attend_jax(zq.reshape(B, T, N_KV, GRP, HEAD_DIM), gates.reshape(B, T, N_KV, GRP, 3), qpos,
                        cmp_kv, full_slc, wkv, wpos)
    return o.reshape(B, T, N_HEADS * HEAD_DIM), wkv[:, -wlen:]


def _relayout_w_in(w):
    o_conv, o_q, o_kv, o_ng = 0, 2048, 3072, 4608
    o_h, o_mg, end = 4656, 8752, 14896
    parts = [w[:, o_mg:end], w[:, o_conv:o_q], w[:, o_q:o_kv], w[:, o_h:o_mg], w[:, o_kv:o_ng], w[:, o_ng:o_h],
             jnp.zeros((w.shape[0], NZ - OFF_NG - 3 * N_HEADS), w.dtype)]
    return jnp.concatenate(parts, axis=1).astype(BF16)


def _route_weights(wg, bg, we, be):
    pad = ROUTE_W - N_EXPERTS - N_GROUPS
    w = jnp.concatenate([we, wg, jnp.zeros((we.shape[0], pad), F32)], axis=1)
    b = jnp.concatenate([be, bg, jnp.zeros((pad,), F32)])[None, :]
    return w, b


def _layer(x, mods, P, l, lb, conv_prev, s0, batch, nsa_state):
    m = x.shape[0]
    t = m // batch
    (sh1, sc1, g1), (sh2, sc2, g2) = mods
    u = modulate(x, sc1, sh1)
    z = matmul(u, P['w_in_r'][l], tn=NZ_TILE)
    z3 = z.reshape(batch, t, NZ)
    conv_w = (P['w_dw'][l], P['b_dw'][l], P['conv_ln_g'][l], P['conv_ln_b'][l])
    cmp_w = (P['w_cmp_pe'][l], P['w_cmp_1'][l], P['b_cmp_1'][l], P['w_cmp_2'][l])
    if nsa_state is None:
        a_conv, conv_new = conv_prompt(z, *conv_w)
        conv_new = conv_new[None]
        o_hg, s_new = hgrn_prompt(z, P['hgrn_lb_logits'], P['hgrn_norm_g'][l][None], l)
        s_new = s_new[None]
        o_nsa, kv6 = nsa_prompt(z, *cmp_w)
        kv6 = kv6[None]
        win_new = kv6[:, -min(WINDOW, t):, :, 4:6]
    else:
        a_conv, conv_new = conv_step(z3, conv_prev, *conv_w)
        o_hg, s_new = hgrn_step(z3, s0, P['hgrn_lb_logits'], P['hgrn_norm_g'][l][None], l)
        pool_l, page_table, win_buf = nsa_state
        o_nsa, kv6 = nsa_sample(z, pool_l, page_table, win_buf, *cmp_w)
        win_new = jnp.concatenate([win_buf, kv6[..., 4:6, :]], axis=1)[:, -win_buf.shape[1]:]
    merged = merge_branches(a_conv.reshape(m, D_CONV), o_nsa, o_hg.reshape(m, -1),
                            P['w_conv_out'][l], P['w_proj_nsa'][l], P['w_proj_hgrn'][l], z)
    y = matmul(merged, P['w_out'][l], tn=512)
    w_r, b_r = _route_weights(P['w_route_group'][l], P['b_route_group'][l], P['w_route_expert'][l], P['b_route_expert'][l])
    x1, u2, comb = ln_residual_route(x, y, g1, P['ln_g'][l, 0][None], P['ln_b'][l, 0][None], sc2, sh2, w_r, b_r)
    y2 = moe_experts(u2, comb, P['w_exp_gate'][l], P['w_exp_up'][l], P['w_exp_down'][l])
    x2 = ln_residual(x1, y2, g2, P['ln_g'][l, 1][None], P['ln_b'][l, 1][None])
    return x2, (kv6[..., 0:4, :], win_new, conv_new, s_new)


def kernel(x_prompt, x_sample, cache_nsa_kv, state_win_kv, state_conv, state_hgrn, page_table,
           c_prompt, c_sample, w_ada, b_ada, w_in, w_cmp_pe, w_cmp_1, b_cmp_1, w_cmp_2,
           w_dw, b_dw, conv_ln_g, conv_ln_b, w_conv_out, hgrn_lb_logits, hgrn_norm_g,
           w_proj_nsa, w_proj_hgrn, w_out, ln_g, ln_b, w_route_group, b_route_group,
           w_route_expert, b_route_expert, w_exp_gate, w_exp_up, w_exp_down):
    P = dict(w_cmp_pe=w_cmp_pe, w_cmp_1=w_cmp_1, b_cmp_1=b_cmp_1, w_cmp_2=w_cmp_2, w_dw=w_dw, b_dw=b_dw,
             conv_ln_g=conv_ln_g, conv_ln_b=conv_ln_b, w_conv_out=w_conv_out, hgrn_norm_g=hgrn_norm_g,
             w_proj_nsa=w_proj_nsa, w_proj_hgrn=w_proj_hgrn, w_out=w_out, ln_g=ln_g, ln_b=ln_b,
             w_route_group=w_route_group, b_route_group=b_route_group, w_route_expert=w_route_expert,
             b_route_expert=b_route_expert, w_exp_gate=w_exp_gate, w_exp_up=w_exp_up, w_exp_down=w_exp_down)
    P['w_in_r'] = [_relayout_w_in(w_in[l]) for l in range(DEPTH)]
    P['hgrn_lb_logits'] = hgrn_lb_logits
    p_lb = jax.nn.softmax(hgrn_lb_logits, axis=0)
    lb_all = jnp.cumsum(p_lb, axis=0) - p_lb[0:1]

    bp, tp = x_prompt.shape[:2]
    bs, ts = x_sample.shape[:2]
    c_all = jnp.concatenate([c_sample, c_prompt, jnp.zeros((8 - bp % 8, D_MODEL), F32)], axis=0)
    ada = ada_all(c_all, w_ada, b_ada)

    def mods(l, i, rows, rep):
        mrow = ada[2 * l + i, rows]
        if rep > 1:
            mrow = jnp.repeat(mrow, rep, axis=0)
        return mrow[:, :D_MODEL], mrow[:, D_MODEL:2 * D_MODEL], mrow[:, 2 * D_MODEL:]

    xp = x_prompt.reshape(bp * tp, D_MODEL)
    xs = x_sample.reshape(bs * ts, D_MODEL)
    outs_p, outs_s = [], []
    for l in range(DEPTH):
        conv0 = jnp.zeros((bp, CONV_W - 1, D_CONV), F32)
        s0 = jnp.zeros((bp, H_HG, DK_HG, DV_HG), F32)
        mp = [mods(l, i, slice(bs, bs + bp), 1) for i in range(2)]
        xp, st = _layer(xp, mp, P, l, lb_all[l], conv0, s0, bp, None)
        outs_p.append(st)
        ms = [mods(l, i, slice(0, bs), ts) for i in range(2)]
        xs, st = _layer(xs, ms, P, l, lb_all[l], state_conv[l], state_hgrn[l], bs,
                        (cache_nsa_kv[l], page_table, state_win_kv[l]))
        outs_s.append(st)
    stack = lambda outs, i: jnp.stack([o[i] for o in outs])
    return (xp.reshape(bp, tp, D_MODEL), xs.reshape(bs, ts, D_MODEL),
            stack(outs_p, 0), stack(outs_p, 1), stack(outs_p, 2), stack(outs_p, 3),
            stack(outs_s, 0), stack(outs_s, 1), stack(outs_s, 2), stack(outs_s, 3))
```

```python
import functools

import numpy as np
import jax
import jax.numpy as jnp
from jax import lax
from jax.experimental import pallas as pl
from jax.experimental.pallas import tpu as pltpu

F32 = jnp.float32
BF16 = jnp.bfloat16

D_MODEL = 2048
DEPTH = 2
D_CONV = 1024
CONV_W = 31
N_HEADS = 16
N_KV = 4
HEAD_DIM = 64
GRP = N_HEADS // N_KV
L_CMP = 32
D_CMP = 16
CMP_HID = 256
BLK_SLC = 64
TOP_N = 16
WINDOW = 512
H_HG = 8
DK_HG = 128
DV_HG = 128
HG_CHUNK = 64
N_GROUPS = 4
EXP_PER_GROUP = 8
N_EXPERTS = N_GROUPS * EXP_PER_GROUP
D_EXPERT = 256
PAGE_SIZE = 128

ALPHA = (2 * DEPTH) ** 0.25
LN_EPS = 1e-5
NEG = -1e30
BIG = 1e9

OFF_MG = 0
OFF_CONV = 3 * D_MODEL
OFF_Q = OFF_CONV + 2 * D_CONV
OFF_HG = OFF_Q + N_HEADS * HEAD_DIM
OFF_KV = OFF_HG + 4 * H_HG * DK_HG
OFF_NG = OFF_KV + N_KV * 6 * HEAD_DIM
NZ = OFF_NG + 128
NZ_TILE = 1152
ROUTE_W = 128


def _params(sem, vmem_mb=48):
    return pltpu.CompilerParams(dimension_semantics=sem, vmem_limit_bytes=vmem_mb * 1024 * 1024)


def _sigmoid(x):
    return 1.0 / (1.0 + jnp.exp(-x))


def _silu(x):
    return x * _sigmoid(x)


def _bdot(a, b):
    return jnp.dot(a.astype(BF16), b.astype(BF16), preferred_element_type=F32)


def _row_spec(arr, tm):
    d = arr.shape[1]
    if arr.shape[0] == 1:
        return pl.BlockSpec((1, d), lambda i: (0, 0))
    return pl.BlockSpec((tm, d), lambda i: (i, 0))


def _mm_kernel(x_ref, w_ref, o_ref):
    o_ref[...] = _bdot(x_ref[...], w_ref[...]).astype(o_ref.dtype)


def matmul(x, w, *, tn, out_dtype=F32, tm=1024):
    m, k = x.shape
    n = w.shape[1]
    tm = min(tm, m)
    return pl.pallas_call(
        _mm_kernel,
        grid=(m // tm, n // tn),
        in_specs=[pl.BlockSpec((tm, k), lambda i, j: (i, 0)),
                  pl.BlockSpec((k, tn), lambda i, j: (0, j))],
        out_specs=pl.BlockSpec((tm, tn), lambda i, j: (i, j)),
        out_shape=jax.ShapeDtypeStruct((m, n), out_dtype),
        compiler_params=_params(("parallel", "arbitrary")),
        name="matmul",
    )(x, w)


def _ada_kernel(c_ref, w_ref, b_ref, o_ref, *, tn):
    j = pl.program_id(1)
    m = _bdot(_silu(c_ref[...]), w_ref[0]) + b_ref[0]
    col = j * tn + lax.broadcasted_iota(jnp.int32, m.shape, 1)
    o_ref[0] = m + (col >= D_MODEL).astype(F32)


def ada_all(c, w_ada, b_ada, tn=1024):
    r = c.shape[0]
    n = 3 * D_MODEL
    w = w_ada.reshape(2 * DEPTH, D_MODEL, n)
    b = b_ada.reshape(2 * DEPTH, 1, n)
    return pl.pallas_call(
        functools.partial(_ada_kernel, tn=tn),
        grid=(2 * DEPTH, n // tn),
        in_specs=[pl.BlockSpec((r, D_MODEL), lambda a, j: (0, 0)),
                  pl.BlockSpec((1, D_MODEL, tn), lambda a, j: (a, 0, j)),
                  pl.BlockSpec((1, 1, tn), lambda a, j: (a, 0, j))],
        out_specs=pl.BlockSpec((1, r, tn), lambda a, j: (a, 0, j)),
        out_shape=jax.ShapeDtypeStruct((2 * DEPTH, r, n), F32),
        compiler_params=_params(("parallel", "arbitrary")),
        name="ada",
    )(c, w, b)


def _mod_kernel(x_ref, sc_ref, sh_ref, o_ref):
    o_ref[...] = (x_ref[...] * sc_ref[...] + sh_ref[...]).astype(o_ref.dtype)


def modulate(x, scale, shift, tm=512):
    m, d = x.shape
    tm = min(tm, m)
    return pl.pallas_call(
        _mod_kernel,
        grid=(m // tm,),
        in_specs=[pl.BlockSpec((tm, d), lambda i: (i, 0)), _row_spec(scale, tm), _row_spec(shift, tm)],
        out_specs=pl.BlockSpec((tm, d), lambda i: (i, 0)),
        out_shape=jax.ShapeDtypeStruct((m, d), BF16),
        compiler_params=_params(("parallel",)),
        name="modulate",
    )(x, scale, shift)


def _post_ln(x_ref, y_ref, gate_ref, g_ref, b_ref):
    v = ALPHA * x_ref[...] + gate_ref[...] * y_ref[...]
    mu = jnp.mean(v, axis=-1, keepdims=True)
    d = v - mu
    var = jnp.mean(d * d, axis=-1, keepdims=True)
    return d * lax.rsqrt(var + LN_EPS) * g_ref[...] + b_ref[...]


def _ln_kernel(x_ref, y_ref, gate_ref, g_ref, b_ref, o_ref):
    o_ref[...] = _post_ln(x_ref, y_ref, gate_ref, g_ref, b_ref)


def _split_bf16(a):
    hi = a.astype(BF16)
    lo = (a - hi.astype(F32)).astype(BF16)
    return hi, lo


def _route(logits):
    lane = lax.broadcasted_iota(jnp.int32, logits.shape, 1).astype(F32)
    far = jnp.float32(1e9)
    is_g = (lane >= N_EXPERTS) & (lane < N_EXPERTS + N_GROUPS)
    gl = jnp.where(is_g, logits, -jnp.inf)
    gmax = jnp.max(gl, axis=1, keepdims=True)
    gidx = jnp.min(jnp.where(gl == gmax, lane, far), axis=1, keepdims=True) - N_EXPERTS
    wg = 1.0 / jnp.sum(jnp.where(is_g, jnp.exp(gl - gmax), 0.0), axis=1, keepdims=True)
    lo = gidx * EXP_PER_GROUP
    in_g = (lane >= lo) & (lane < lo + EXP_PER_GROUP)
    el = jnp.where(in_g, logits, -jnp.inf)
    m1 = jnp.max(el, axis=1, keepdims=True)
    i1 = jnp.min(jnp.where(el == m1, lane, far), axis=1, keepdims=True)
    el2 = jnp.where(lane == i1, -jnp.inf, el)
    m2 = jnp.max(el2, axis=1, keepdims=True)
    i2 = jnp.min(jnp.where(el2 == m2, lane, far), axis=1, keepdims=True)
    e2 = jnp.exp(m2 - m1)
    t1 = 1.0 / (1.0 + e2)
    t2 = e2 / (1.0 + e2)
    return jnp.where(lane == i1, wg * t1, 0.0) + jnp.where(lane == i2, wg * t2, 0.0)


def _ln_route_kernel(x_ref, y_ref, gate_ref, g_ref, b_ref, sc_ref, sh_ref, wr_ref, br_ref,
                     xn_ref, u_ref, comb_ref):
    xn = _post_ln(x_ref, y_ref, gate_ref, g_ref, b_ref)
    xn_ref[...] = xn
    u = xn * sc_ref[...] + sh_ref[...]
    u_ref[...] = u.astype(BF16)
    u_hi, u_lo = _split_bf16(u)
    w_hi, w_lo = _split_bf16(wr_ref[...])
    logits = (jnp.dot(u_hi, w_hi, preferred_element_type=F32)
              + jnp.dot(u_lo, w_hi, preferred_element_type=F32)
              + jnp.dot(u_hi, w_lo, preferred_element_type=F32)) + br_ref[...]
    comb_ref[...] = _route(logits)


def ln_residual(x, y, gate, g, b, tm=512):
    m, d = x.shape
    tm = min(tm, m)
    row = pl.BlockSpec((tm, d), lambda i: (i, 0))
    vec = pl.BlockSpec((1, d), lambda i: (0, 0))
    return pl.pallas_call(
        _ln_kernel,
        grid=(m // tm,),
        in_specs=[row, row, _row_spec(gate, tm), vec, vec],
        out_specs=row,
        out_shape=jax.ShapeDtypeStruct((m, d), F32),
        compiler_params=_params(("parallel",)),
        name="ln_residual",
    )(x, y, gate, g, b)


def ln_residual_route(x, y, gate, g, b, scale, shift, w_route, b_route, tm=512):
    m, d = x.shape
    tm = min(tm, m)
    row = pl.BlockSpec((tm, d), lambda i: (i, 0))
    vec = pl.BlockSpec((1, d), lambda i: (0, 0))
    return pl.pallas_call(
        _ln_route_kernel,
        grid=(m // tm,),
        in_specs=[row, row, _row_spec(gate, tm), vec, vec, _row_spec(scale, tm), _row_spec(shift, tm),
                  pl.BlockSpec((d, ROUTE_W), lambda i: (0, 0)), pl.BlockSpec((1, ROUTE_W), lambda i: (0, 0))],
        out_specs=[row, row, pl.BlockSpec((tm, ROUTE_W), lambda i: (i, 0))],
        out_shape=[jax.ShapeDtypeStruct((m, d), F32), jax.ShapeDtypeStruct((m, d), BF16),
                   jax.ShapeDtypeStruct((m, ROUTE_W), F32)],
        compiler_params=_params(("parallel",)),
        name="ln_residual_route",
    )(x, y, gate, g, b, scale, shift, w_route, b_route)


def _merge_kernel(ac_ref, on_ref, oh_ref, wc_ref, wn_ref, wh_ref, za_ref, zb_ref, zc_ref, o_ref):
    yc = _bdot(ac_ref[...], wc_ref[...])
    yn = _bdot(on_ref[...], wn_ref[...])
    yh = _bdot(oh_ref[...], wh_ref[...])
    o = _sigmoid(za_ref[...]) * yc + _sigmoid(zb_ref[...]) * yn + _sigmoid(zc_ref[...]) * yh
    o_ref[...] = o.astype(o_ref.dtype)


def merge_branches(a_conv, o_nsa, o_hg, w_conv_out, w_proj_nsa, w_proj_hgrn, z, tm=512, tn=512):
    m, k = a_conv.shape
    tm = min(tm, m)
    nb = D_MODEL // tn
    act = pl.BlockSpec((tm, k), lambda i, j: (i, 0))
    wsp = pl.BlockSpec((k, tn), lambda i, j: (0, j))
    gate = [pl.BlockSpec((tm, tn), lambda i, j, o=(OFF_MG + c * D_MODEL) // tn: (i, o + j)) for c in range(3)]
    return pl.pallas_call(
        _merge_kernel,
        grid=(m // tm, nb),
        in_specs=[act, act, act, wsp, wsp, wsp] + gate,
        out_specs=pl.BlockSpec((tm, tn), lambda i, j: (i, j)),
        out_shape=jax.ShapeDtypeStruct((m, D_MODEL), BF16),
        compiler_params=_params(("parallel", "arbitrary")),
        name="merge_branches",
    )(a_conv, o_nsa, o_hg, w_conv_out, w_proj_nsa, w_proj_hgrn, z, z, z)


def _moe_kernel(u_ref, comb_ref, wg_ref, wu_ref, wd_ref, o_ref):
    e = pl.program_id(1)

    @pl.when(e == 0)
    def _():
        o_ref[...] = jnp.zeros_like(o_ref)

    u = u_ref[...]
    a = _bdot(u, wg_ref[0])
    b = _bdot(u, wu_ref[0])
    comb = comb_ref[...]
    lane = lax.broadcasted_iota(jnp.int32, comb.shape, 1)
    c = jnp.sum(jnp.where(lane == e, comb, 0.0), axis=1, keepdims=True)
    h = _silu(a) * b * c
    o_ref[...] += _bdot(h, wd_ref[0])


def moe_experts(u, comb, w_gate, w_up, w_down, tm=1024):
    m, d = u.shape
    tm = min(tm, m)
    return pl.pallas_call(
        _moe_kernel,
        grid=(m // tm, N_EXPERTS),
        in_specs=[pl.BlockSpec((tm, d), lambda i, e: (i, 0)),
                  pl.BlockSpec((tm, ROUTE_W), lambda i, e: (i, 0)),
                  pl.BlockSpec((1, d, D_EXPERT), lambda i, e: (e, 0, 0)),
                  pl.BlockSpec((1, d, D_EXPERT), lambda i, e: (e, 0, 0)),
                  pl.BlockSpec((1, D_EXPERT, d), lambda i, e: (e, 0, 0))],
        out_specs=pl.BlockSpec((tm, d), lambda i, e: (i, 0)),
        out_shape=jax.ShapeDtypeStruct((m, d), F32),
        compiler_params=_params(("parallel", "arbitrary")),
        name="moe_experts",
    )(u, comb, w_gate, w_up, w_down)


def _compress_kernel(x_ref, pa_ref, pb_ref, wa_ref, wb_ref, b1_ref, w2_ref, o_ref):
    x = x_ref[0]
    rows = x.shape[0]
    ha = _bdot(x + pa_ref[0], wa_ref[0])
    hb = _bdot(x + pb_ref[0], wb_ref[0])
    hb_next = pltpu.roll(hb, rows - 1, 0)
    h = _silu(ha + hb_next + b1_ref[0])
    o_ref[0] = _bdot(h, w2_ref[0])


def nsa_compress_blocks(x, pe, w1, b1, w2):
    rows = x.shape[1]
    kd = D_CMP * HEAD_DIM
    pa = pe[:, :D_CMP].reshape(2, 1, kd)
    pb = pe[:, D_CMP:].reshape(2, 1, kd)
    wa = w1[:, :D_CMP].reshape(2, kd, CMP_HID)
    wb = w1[:, D_CMP:].reshape(2, kd, CMP_HID)
    s_of = lambda i: (i % 2, 0, 0)
    return pl.pallas_call(
        _compress_kernel,
        grid=(2 * N_KV,),
        in_specs=[pl.BlockSpec((1, rows, kd), lambda i: (i, 0, 0)),
                  pl.BlockSpec((1, 1, kd), s_of), pl.BlockSpec((1, 1, kd), s_of),
                  pl.BlockSpec((1, kd, CMP_HID), s_of), pl.BlockSpec((1, kd, CMP_HID), s_of),
                  pl.BlockSpec((1, 1, CMP_HID), s_of), pl.BlockSpec((1, CMP_HID, HEAD_DIM), s_of)],
        out_specs=pl.BlockSpec((1, rows, HEAD_DIM), lambda i: (i, 0, 0)),
        out_shape=jax.ShapeDtypeStruct((2 * N_KV, rows, HEAD_DIM), F32),
        compiler_params=_params(("parallel",)),
        name="nsa_compress",
    )(x, pa, pb, wa, wb, b1.reshape(2, 1, CMP_HID), w2)


def _cmp_kernel(qT_ref, kc_ref, vcT_ref, ov_ref, o_ref, bias_ref, *, tq, k_sel):
    qb = pl.program_id(1)
    kc = kc_ref[0]
    vcT = vcT_ref[0]
    nblk = kc.shape[0]
    n_slc = ov_ref.shape[0]
    tpos = qb * tq + lax.broadcasted_iota(jnp.int32, (nblk, tq), 1)
    last = lax.broadcasted_iota(jnp.int32, (nblk, tq), 0) * D_CMP + (L_CMP - 1)
    vis = last <= tpos
    visf = vis.astype(F32)
    psum = jnp.zeros((nblk, tq), F32)
    for r in range(GRP):
        s = jnp.dot(kc, qT_ref[r], preferred_element_type=F32)
        s = jnp.where(vis, s, NEG)
        e = jnp.exp(s - jnp.max(s, axis=0, keepdims=True))
        p = e / jnp.sum(e, axis=0, keepdims=True) * visf
        o_ref[r] = jnp.dot(vcT, p.astype(BF16), preferred_element_type=F32)
        psum = psum + p
    p_hi, p_lo = _split_bf16(psum)
    ov = ov_ref[...]
    imp = jnp.dot(ov, p_hi, preferred_element_type=F32) + jnp.dot(ov, p_lo, preferred_element_type=F32)
    j = lax.broadcasted_iota(jnp.int32, (n_slc, tq), 0)
    cur = (qb * tq + lax.broadcasted_iota(jnp.int32, (n_slc, tq), 1)) // BLK_SLC
    valid = j <= cur
    forced = (j == 0) | (j == cur) | (j == cur - 1)
    score = jnp.where(forced, BIG, jnp.where(valid, imp, -BIG))
    jf = j.astype(F32)
    sel = jnp.zeros((n_slc, tq), F32)
    for _ in range(k_sel):
        m = jnp.max(score, axis=0, keepdims=True)
        idx = jnp.min(jnp.where(score == m, jf, 1e9), axis=0, keepdims=True)
        hit = jf == idx
        sel = jnp.where(hit, 1.0, sel)
        score = jnp.where(hit, -3e38, score)
    bias_ref[0] = jnp.where((sel > 0.0) & valid, 0.0, NEG).astype(BF16)


def _online_softmax_step(s, v, r, m_s, l_s, acc_s):
    m_old = m_s[r]
    m_new = jnp.maximum(m_old, jnp.max(s, axis=0, keepdims=True))
    alpha = jnp.exp(m_old - m_new)
    p = jnp.exp(s - m_new)
    l_s[r] = alpha * l_s[r] + jnp.sum(p, axis=0, keepdims=True)
    acc_s[r] = alpha * acc_s[r] + jnp.dot(v, p.astype(BF16), preferred_element_type=F32)
    m_s[r] = m_new


def _slc_kernel(qT_ref, bias_ref, k_ref, vT_ref, o_ref, rhs_s, m_s, l_s, acc_s, *, tq):
    qb = pl.program_id(1)
    m_s[...] = jnp.full(m_s.shape, NEG, F32)
    l_s[...] = jnp.zeros(l_s.shape, F32)
    acc_s[...] = jnp.zeros(acc_s.shape, F32)
    for r in range(GRP):
        rhs_s[0:HEAD_DIM, r * tq:(r + 1) * tq] = qT_ref[r]
        rhs_s[HEAD_DIM:, r * tq:(r + 1) * tq] = bias_ref[0]

    def tile(kt, diagonal):
        s = jnp.dot(k_ref[0, kt], rhs_s[...], preferred_element_type=F32)
        if diagonal:
            kpos = kt * tq + lax.broadcasted_iota(jnp.int32, (tq, GRP * tq), 0)
            tpos = qb * tq + lax.broadcasted_iota(jnp.int32, (tq, GRP * tq), 1) % tq
            s = jnp.where(kpos <= tpos, s, NEG)
        m_old = m_s[...]
        m_new = jnp.maximum(m_old, jnp.max(s, axis=0, keepdims=True))
        alpha = jnp.exp(m_old - m_new)
        p = jnp.exp(s - m_new)
        l_s[...] = alpha * l_s[...] + jnp.sum(p, axis=0, keepdims=True)
        acc_s[...] = alpha * acc_s[...] + jnp.dot(vT_ref[0, kt], p.astype(BF16), preferred_element_type=F32)
        m_s[...] = m_new

    def body(kt, carry):
        tile(kt, False)
        return carry

    lax.fori_loop(0, qb, body, 0)
    tile(qb, True)
    o = acc_s[...] / l_s[...]
    for r in range(GRP):
        o_ref[r] = o[:, r * tq:(r + 1) * tq]


def _win_kernel(qT_ref, k_ref, vT_ref, oc_ref, os_ref, zng_ref, o_ref, m_s, l_s, acc_s, *, tq):
    qb = pl.program_id(1)
    m_s[...] = jnp.full(m_s.shape, NEG, F32)
    l_s[...] = jnp.zeros(l_s.shape, F32)
    acc_s[...] = jnp.zeros(acc_s.shape, F32)
    tpos = qb * tq + lax.broadcasted_iota(jnp.int32, (tq, tq), 1)
    row = lax.broadcasted_iota(jnp.int32, (tq, tq), 0)
    for i in range(WINDOW // tq + 1):
        kt_raw = qb - i
        kt = jnp.maximum(kt_raw, 0)
        k = k_ref[0, kt]
        v = vT_ref[0, kt]
        kpos = kt_raw * tq + row
        ok = (kpos <= tpos) & (kpos > tpos - WINDOW) & (kpos >= 0)
        for r in range(GRP):
            s = jnp.dot(k, qT_ref[r], preferred_element_type=F32)
            _online_softmax_step(jnp.where(ok, s, NEG), v, r, m_s, l_s, acc_s)
    zng = zng_ref[0]
    for r in range(GRP):
        o_win = acc_s[r] / l_s[r]
        g_cmp = _sigmoid(zng[r:r + 1, :])
        g_slc = _sigmoid(zng[GRP + r:GRP + r + 1, :])
        g_win = _sigmoid(zng[2 * GRP + r:2 * GRP + r + 1, :])
        o_ref[r] = (g_cmp * oc_ref[r] + g_slc * os_ref[r] + g_win * o_win).astype(o_ref.dtype)


def _tiles_rows(a, tq):
    g, t, c = a.shape
    return a.reshape(g, t // tq, tq, c)


def _tiles_cols(a, tq):
    g, c, t = a.shape
    return a.reshape(g, c, t // tq, tq).transpose(0, 2, 1, 3)


def nsa_prompt(z, pe, w1, b1, w2, tq=256):
    t = z.shape[0]
    nblk = t // D_CMP
    n_slc = t // BLK_SLC
    k_sel = min(TOP_N, n_slc)
    nq = t // tq
    zq = z[:, OFF_Q:OFF_Q + N_HEADS * HEAD_DIM]
    kv6 = z[:, OFF_KV:OFF_KV + N_KV * 6 * HEAD_DIM].reshape(t, N_KV, 6, HEAD_DIM)
    zng = z[:, OFF_NG:OFF_NG + 3 * N_HEADS]
    qT = (zq.reshape(t, N_HEADS, HEAD_DIM).transpose(1, 2, 0) * (HEAD_DIM ** -0.5)).astype(BF16)
    x = kv6[:, :, 0:2].reshape(nblk, D_CMP, N_KV, 2, HEAD_DIM).transpose(2, 3, 0, 1, 4)
    cmp = nsa_compress_blocks(x.reshape(2 * N_KV, nblk, D_CMP * HEAD_DIM), pe, w1, b1, w2)
    kc = cmp[0::2].astype(BF16)
    vcT = cmp[1::2].transpose(0, 2, 1).astype(BF16)
    i0 = np.arange(nblk)[None, :] * D_CMP
    j0 = np.arange(n_slc)[:, None] * BLK_SLC
    ovT = jnp.asarray(((i0 < j0 + BLK_SLC) & (i0 + L_CMP > j0)).astype(np.float32), dtype=BF16)

    head_blk = pl.BlockSpec((GRP, HEAD_DIM, tq), lambda g, i: (g, 0, i))
    scratch = [pltpu.VMEM((GRP, 1, tq), F32), pltpu.VMEM((GRP, 1, tq), F32), pltpu.VMEM((GRP, HEAD_DIM, tq), F32)]
    o_cmp, bias = pl.pallas_call(
        functools.partial(_cmp_kernel, tq=tq, k_sel=k_sel),
        grid=(N_KV, nq),
        in_specs=[head_blk,
                  pl.BlockSpec((1, nblk, HEAD_DIM), lambda g, i: (g, 0, 0)),
                  pl.BlockSpec((1, HEAD_DIM, nblk), lambda g, i: (g, 0, 0)),
                  pl.BlockSpec((n_slc, nblk), lambda g, i: (0, 0))],
        out_specs=[head_blk, pl.BlockSpec((1, n_slc, tq), lambda g, i: (g, 0, i))],
        out_shape=[jax.ShapeDtypeStruct((N_HEADS, HEAD_DIM, t), F32),
                   jax.ShapeDtypeStruct((N_KV, n_slc, t), BF16)],
        compiler_params=_params(("parallel", "arbitrary")),
        name="nsa_cmp_select",
    )(qT, kc, vcT, ovT)

    k_slc = kv6[:, :, 2].transpose(1, 0, 2)
    onehot = (jnp.arange(t)[:, None] // BLK_SLC == jnp.arange(n_slc)[None, :]).astype(F32)
    k_aug = jnp.concatenate([k_slc, jnp.broadcast_to(onehot, (N_KV, t, n_slc))], axis=-1).astype(BF16)
    v_slcT = kv6[:, :, 3].transpose(1, 2, 0).astype(BF16)
    ka = HEAD_DIM + n_slc
    o_slc = pl.pallas_call(
        functools.partial(_slc_kernel, tq=tq),
        grid=(N_KV, nq),
        in_specs=[head_blk,
                  pl.BlockSpec((1, n_slc, tq), lambda g, i: (g, 0, i)),
                  pl.BlockSpec((1, nq, tq, ka), lambda g, i: (g, 0, 0, 0)),
                  pl.BlockSpec((1, nq, HEAD_DIM, tq), lambda g, i: (g, 0, 0, 0))],
        out_specs=head_blk,
        out_shape=jax.ShapeDtypeStruct((N_HEADS, HEAD_DIM, t), F32),
        scratch_shapes=[pltpu.VMEM((ka, GRP * tq), BF16), pltpu.VMEM((1, GRP * tq), F32),
                        pltpu.VMEM((1, GRP * tq), F32), pltpu.VMEM((HEAD_DIM, GRP * tq), F32)],
        compiler_params=_params(("parallel", "arbitrary")),
        name="nsa_selected",
    )(qT, bias, _tiles_rows(k_aug, tq), _tiles_cols(v_slcT, tq))

    k_win = kv6[:, :, 4].transpose(1, 0, 2).astype(BF16)
    v_winT = kv6[:, :, 5].transpose(1, 2, 0).astype(BF16)
    zngT = zng.reshape(t, N_KV, GRP, 3).transpose(1, 3, 2, 0).reshape(N_KV, 3 * GRP, t)
    oT = pl.pallas_call(
        functools.partial(_win_kernel, tq=tq),
        grid=(N_KV, nq),
        in_specs=[head_blk,
                  pl.BlockSpec((1, nq, tq, HEAD_DIM), lambda g, i: (g, 0, 0, 0)),
                  pl.BlockSpec((1, nq, HEAD_DIM, tq), lambda g, i: (g, 0, 0, 0)),
                  head_blk, head_blk,
                  pl.BlockSpec((1, 3 * GRP, tq), lambda g, i: (g, 0, i))],
        out_specs=head_blk,
        out_shape=jax.ShapeDtypeStruct((N_HEADS, HEAD_DIM, t), BF16),
        scratch_shapes=scratch,
        compiler_params=_params(("parallel", "arbitrary")),
        name="nsa_window_combine",
    )(qT, _tiles_rows(k_win, tq), _tiles_cols(v_winT, tq), o_cmp, o_slc, zngT)
    o = oT.transpose(2, 0, 1).reshape(t, N_HEADS * HEAD_DIM)
    return o, kv6


def _nt_dot(a, b):
    return lax.dot_general(a.astype(BF16), b.astype(BF16), (((1,), (1,)), ((), ())), preferred_element_type=F32)


def _nsa_sample_kernel(pt_ref, q_ref, zng_ref, kns_ref, knw_ref, win_ref, *rest, n_pages, past, t_new):
    cmp_pages = rest[:n_pages]
    slc_pages = rest[n_pages:2 * n_pages]
    pa_ref, pb_ref, wa_ref, wb_ref, b1_ref, w2_ref, ov_ref, ex_ref, o_ref = rest[2 * n_pages:]
    del pt_ref
    rows = GRP * t_new
    nblk = n_pages * (PAGE_SIZE // D_CMP)
    n_cmp = nblk - 1
    n_slc = -(-(past + t_new) // BLK_SLC)
    k_sel = min(TOP_N, n_slc)
    wlen = win_ref.shape[-1]
    lanes = 128
    lane = lax.broadcasted_iota(jnp.int32, (rows, lanes), 1)
    tok = lax.broadcasted_iota(jnp.int32, (rows, lanes), 0) % t_new
    qpos = past + tok
    wlane = lax.broadcasted_iota(jnp.int32, (rows, wlen), 1)
    wtok = lax.broadcasted_iota(jnp.int32, (rows, wlen), 0) % t_new

    def pe_term(p_ref, w_ref, s):
        hi, lo = _split_bf16(jnp.broadcast_to(p_ref[s], (8, p_ref.shape[-1])))
        w = w_ref[s]
        return (jnp.dot(hi, w, preferred_element_type=F32) + jnp.dot(lo, w, preferred_element_type=F32))[0:1]

    cmp = []
    for s in range(2):
        x = jnp.concatenate([cmp_pages[p][0, 0, g * 2 + s] for g in range(N_KV) for p in range(n_pages)], axis=0)
        ha = jnp.dot(x, wa_ref[s], preferred_element_type=F32) + pe_term(pa_ref, wa_ref, s)
        hb = jnp.dot(x, wb_ref[s], preferred_element_type=F32) + pe_term(pb_ref, wb_ref, s)
        h = _silu(ha + pltpu.roll(hb, N_KV * nblk - 1, 0) + b1_ref[s])
        cmp.append(_bdot(h, w2_ref[s]))

    new_ok = (lane < t_new) & (lane <= tok)
    zeros_tail = jnp.zeros((lanes - kns_ref.shape[2], lanes), BF16)
    for g in range(N_KV):
        q = q_ref[0, g]
        kc = cmp[0][g * nblk:(g + 1) * nblk]
        vc = cmp[1][g * nblk:(g + 1) * nblk]
        vis = (lane * D_CMP + (L_CMP - 1) <= qpos) & (lane < n_cmp)
        s = jnp.where(vis, _nt_dot(q[:, :HEAD_DIM], kc), NEG)
        e = jnp.exp(s - jnp.max(s, axis=1, keepdims=True))
        p = e / jnp.sum(e, axis=1, keepdims=True) * vis.astype(F32)
        o_cmp = _bdot(p, vc)
        psum = p
        for r in range(1, GRP):
            psum = psum + pltpu.roll(p, r * t_new, 0)
        p_hi, p_lo = _split_bf16(psum)
        ov = ov_ref[...]
        imp = jnp.dot(p_hi, ov, preferred_element_type=F32) + jnp.dot(p_lo, ov, preferred_element_type=F32)
        cur = qpos // BLK_SLC
        valid = (lane <= cur) & (lane < n_slc)
        forced = (lane == 0) | (lane == cur) | (lane == cur - 1)
        score = jnp.where(lane < n_slc, jnp.where(forced, BIG, jnp.where(valid, imp, -BIG)), -3e38)
        rank = jnp.zeros((rows, lanes), F32)
        for sh in range(1, n_slc):
            lower = pltpu.roll(score, sh, 1)
            upper = pltpu.roll(score, lanes - sh, 1)
            rank = rank + (lower >= score).astype(F32) + (upper > score).astype(F32)
        sel = (rank < k_sel) & valid
        bias = _bdot(jnp.where(sel, 0.0, NEG), ex_ref[...])
        qh = q[:, :HEAD_DIM]
        kv_new = jnp.concatenate([kns_ref[0, g], zeros_tail], axis=0)
        sc = [_bdot(qh, slc_pages[p][0, 0, g, 0]) + bias[:, p * lanes:(p + 1) * lanes] for p in range(n_pages)]
        s_new = jnp.where(new_ok, _nt_dot(q, kv_new) + bias[:, n_pages * lanes:(n_pages + 1) * lanes], NEG)
        m = s_new.max(axis=1, keepdims=True)
        for x in sc:
            m = jnp.maximum(m, x.max(axis=1, keepdims=True))
        e = jnp.exp(s_new - m)
        den = jnp.sum(e, axis=1, keepdims=True)
        acc = _bdot(e, kv_new)[:, HEAD_DIM:]
        for p in range(n_pages):
            e = jnp.exp(sc[p] - m)
            den = den + jnp.sum(e, axis=1, keepdims=True)
            acc = acc + _nt_dot(e, slc_pages[p][0, 0, g, 1])
        o_slc = acc / den
        kw_new = jnp.concatenate([knw_ref[0, g], zeros_tail], axis=0)
        wpos = past - wlen + wlane
        wq = past + wtok
        ok = (wpos <= wq) & (wpos > wq - WINDOW) & (wpos >= 0)
        s_old = jnp.where(ok, _bdot(qh, win_ref[0, 0, g, 0]), NEG)
        s_new = jnp.where(new_ok, _nt_dot(q, kw_new), NEG)
        m = jnp.maximum(s_old.max(axis=1, keepdims=True), s_new.max(axis=1, keepdims=True))
        e_old = jnp.exp(s_old - m)
        e_new = jnp.exp(s_new - m)
        den = jnp.sum(e_old, axis=1, keepdims=True) + jnp.sum(e_new, axis=1, keepdims=True)
        o_win = (_nt_dot(e_old, win_ref[0, 0, g, 1]) + _bdot(e_new, kw_new)[:, HEAD_DIM:]) / den
        gates = _sigmoid(zng_ref[0, g])
        o_ref[0, g] = gates[:, 0:1] * o_cmp + gates[:, 1:2] * o_slc + gates[:, 2:3] * o_win


def nsa_sample(z, pool_cmp, pool_t, win_t, layer, page_table, pe, w1, b1, w2):
    bsz, n_pages = page_table.shape
    t_new = z.shape[0] // bsz
    past = n_pages * PAGE_SIZE
    wlen = win_t.shape[-1]
    rows = GRP * t_new
    nblk = past // D_CMP
    n_slc = -(-(past + t_new) // BLK_SLC)
    lanes = 128
    assert nblk == lanes and n_slc <= lanes and wlen % lanes == 0 and t_new <= 8

    def per_group(a, width):
        return a.reshape(bsz, t_new, N_KV, GRP, width).transpose(0, 2, 3, 1, 4).reshape(bsz, N_KV, rows, width)

    zq = per_group(z[:, OFF_Q:OFF_Q + N_HEADS * HEAD_DIM], HEAD_DIM) * (HEAD_DIM ** -0.5)
    q_pad = jnp.pad(zq, ((0, 0), (0, 0), (0, 0), (0, lanes - HEAD_DIM))).astype(BF16)
    zng = jnp.pad(per_group(z[:, OFF_NG:OFF_NG + 3 * N_HEADS], 3), ((0, 0), (0, 0), (0, 0), (0, lanes - 3)))
    kv6 = z[:, OFF_KV:OFF_KV + N_KV * 6 * HEAD_DIM].reshape(bsz, t_new, N_KV, 6, HEAD_DIM)

    def new_rows(c0):
        a = kv6[:, :, :, c0:c0 + 2].transpose(0, 2, 1, 3, 4).reshape(bsz, N_KV, t_new, 2 * HEAD_DIM)
        return jnp.pad(a, ((0, 0), (0, 0), (0, 8 - t_new), (0, 0))).astype(BF16)

    per_page = PAGE_SIZE // D_CMP
    kd = D_CMP * HEAD_DIM
    pa = pe[:, :D_CMP].reshape(2, 1, kd)
    pb = pe[:, D_CMP:].reshape(2, 1, kd)
    wa = w1[:, :D_CMP].reshape(2, kd, CMP_HID).astype(BF16)
    wb = w1[:, D_CMP:].reshape(2, kd, CMP_HID).astype(BF16)
    i0 = np.arange(lanes)[:, None] * D_CMP
    j0 = np.arange(lanes)[None, :] * BLK_SLC
    ov = ((i0 < j0 + BLK_SLC) & (i0 + L_CMP > j0) & (np.arange(lanes)[:, None] < nblk - 1)
          & (np.arange(lanes)[None, :] < n_slc))
    ov = jnp.asarray(ov.astype(np.float32), dtype=BF16)
    nkeys = (n_pages + 1) * lanes
    ex = jnp.asarray((np.arange(nkeys)[None, :] // BLK_SLC == np.arange(lanes)[:, None]).astype(np.float32), dtype=BF16)

    full = lambda shape: pl.BlockSpec(shape, lambda b, pt: (0,) * len(shape))
    per_b = lambda shape: pl.BlockSpec((1,) + shape, lambda b, pt: (b,) + (0,) * len(shape))
    cmp_page = lambda p: pl.BlockSpec((1, 1, 2 * N_KV, per_page, kd), lambda b, pt: (layer, pt[b, p], 0, 0, 0))
    slc_page = lambda p: pl.BlockSpec((1, 1, N_KV, 2, HEAD_DIM, PAGE_SIZE), lambda b, pt: (layer, pt[b, p], 0, 1, 0, 0))
    in_specs = ([per_b((N_KV, rows, lanes)), per_b((N_KV, rows, lanes)), per_b((N_KV, 8, lanes)),
                 per_b((N_KV, 8, lanes)),
                 pl.BlockSpec((1, 1, N_KV, 2, HEAD_DIM, wlen), lambda b, pt: (layer, b, 0, 0, 0, 0))]
                + [cmp_page(p) for p in range(n_pages)]
                + [slc_page(p) for p in range(n_pages)]
                + [full((2, 1, kd)), full((2, 1, kd)), full((2, kd, CMP_HID)), full((2, kd, CMP_HID)),
                   full((2, 1, CMP_HID)), full((2, CMP_HID, HEAD_DIM)), full((lanes, lanes)), full((lanes, nkeys))])
    o = pl.pallas_call(
        functools.partial(_nsa_sample_kernel, n_pages=n_pages, past=past, t_new=t_new),
        grid_spec=pltpu.PrefetchScalarGridSpec(
            num_scalar_prefetch=1, grid=(bsz,), in_specs=in_specs,
            out_specs=pl.BlockSpec((1, N_KV, rows, HEAD_DIM), lambda b, pt: (b, 0, 0, 0))),
        out_shape=jax.ShapeDtypeStruct((bsz, N_KV, rows, HEAD_DIM), F32),
        compiler_params=_params(("arbitrary",)),
        name="nsa_sample",
    )(page_table, q_pad, zng, new_rows(2), new_rows(4), win_t,
      *([pool_cmp] * n_pages), *([pool_t] * n_pages),
      pa, pb, wa, wb, b1.reshape(2, 1, CMP_HID), w2.astype(BF16), ov, ex)
    o = o.reshape(bsz, N_KV, GRP, t_new, HEAD_DIM).transpose(0, 3, 1, 2, 4).reshape(bsz * t_new, N_HEADS * HEAD_DIM)
    return o, kv6


HG_SUB = 16
HG_BLOCK = 256


def _lower_bound(lbl, layer):
    e = jnp.exp(lbl - jnp.max(lbl, axis=0, keepdims=True))
    p = e / jnp.sum(e, axis=0, keepdims=True)
    lb = jnp.zeros((1, lbl.shape[1]), F32)
    for i in range(1, layer + 1):
        lb = lb + p[i:i + 1, :]
    return lb


def _hgrn_kernel(zq_ref, zf_ref, zi_ref, zg_ref, lbl_ref, ng_ref, ltri_ref, o_ref, s_ref, st_s, *, layer):
    i = pl.program_id(1)
    tb = zq_ref.shape[0]
    c = HG_SUB
    nc = tb // c

    @pl.when(i == 0)
    def _():
        st_s[...] = jnp.zeros(st_s.shape, F32)

    lb = _lower_bound(lbl_ref[...], layer)
    q = _silu(zq_ref[...])
    f = lb + (1.0 - lb) * _sigmoid(zf_ref[...])
    k = 1.0 - f
    v = zi_ref[...]
    logf = jnp.log(f)
    h1 = logf.astype(BF16)
    r1 = logf - h1.astype(F32)
    h2 = r1.astype(BF16)
    h3 = (r1 - h2.astype(F32)).astype(BF16)
    ltri = ltri_ref[...]
    g = (jnp.dot(ltri, h1, preferred_element_type=F32) + jnp.dot(ltri, h2, preferred_element_type=F32)
         + jnp.dot(ltri, h3, preferred_element_type=F32))
    g3 = g.reshape(nc, c, DK_HG)
    q3 = q.reshape(nc, c, DK_HG)
    k3 = k.reshape(nc, c, DK_HG)
    v3 = v.reshape(nc, c, DV_HG)
    g_last = g3[:, c - 1:c, :]
    qe = (q * jnp.exp(g)).astype(BF16)
    kd = (k3 * jnp.exp(g_last - g3)).reshape(tb, DK_HG).astype(BF16)
    dec = jnp.exp(g_last)

    tpos = lax.broadcasted_iota(jnp.int32, (nc, c, DK_HG), 1)
    o_intra = jnp.zeros((nc, c, DV_HG), F32)
    for s in range(c):
        d = jnp.exp(jnp.where(tpos >= s, g3 - g3[:, s:s + 1, :], 0.0))
        w = jnp.where(tpos >= s, q3 * k3[:, s:s + 1, :] * d, 0.0)
        o_intra = o_intra + jnp.sum(w, axis=-1, keepdims=True) * v3[:, s:s + 1, :]

    vT = v.T
    lane_chunk = lax.broadcasted_iota(jnp.int32, (DV_HG, tb), 1) // c
    st = st_s[...]
    o_inter = []
    for ci in range(nc):
        o_inter.append(lax.dot_general(qe[ci * c:(ci + 1) * c, :], st.astype(BF16), (((1,), (1,)), ((), ())),
                                       preferred_element_type=F32))
        ut = jnp.dot(jnp.where(lane_chunk == ci, vT, 0.0).astype(BF16), kd, preferred_element_type=F32)
        st = st * dec[ci] + ut
    st_s[...] = st
    o = jnp.concatenate(o_inter, axis=0) + o_intra.reshape(tb, DV_HG)
    o = o * lax.rsqrt(jnp.mean(o * o, axis=-1, keepdims=True) + LN_EPS) * ng_ref[...]
    o_ref[...] = (o * _silu(zg_ref[...])).astype(o_ref.dtype)

    @pl.when(i == pl.num_programs(1) - 1)
    def _():
        s_ref[0] = st.T


def hgrn_prompt(z, lb_logits, norm_g, layer):
    t = z.shape[0]
    tb = min(HG_BLOCK, t)
    r = np.arange(tb)
    ltri = jnp.asarray(((r[:, None] // HG_SUB == r[None, :] // HG_SUB) & (r[None, :] <= r[:, None])).astype(np.float32),
                       dtype=BF16)
    col = lambda cidx: pl.BlockSpec((tb, DK_HG), lambda h, i, o=(OFF_HG + cidx * H_HG * DK_HG) // DK_HG: (i, o + h))
    return pl.pallas_call(
        functools.partial(_hgrn_kernel, layer=layer),
        grid=(H_HG, t // tb),
        in_specs=[col(0), col(1), col(2), col(3),
                  pl.BlockSpec((DEPTH, DK_HG), lambda h, i: (0, h)),
                  pl.BlockSpec((1, DV_HG), lambda h, i: (0, h)),
                  pl.BlockSpec((tb, tb), lambda h, i: (0, 0))],
        out_specs=[pl.BlockSpec((tb, DV_HG), lambda h, i: (i, h)),
                   pl.BlockSpec((1, DK_HG, DV_HG), lambda h, i: (h, 0, 0))],
        out_shape=[jax.ShapeDtypeStruct((t, H_HG * DV_HG), BF16),
                   jax.ShapeDtypeStruct((H_HG, DK_HG, DV_HG), F32)],
        scratch_shapes=[pltpu.VMEM((DV_HG, DK_HG), F32)],
        compiler_params=_params(("parallel", "arbitrary")),
        name="hgrn_prompt",
    )(z, z, z, z, lb_logits, norm_g, ltri)


def _hgrn_step_kernel(zq_ref, zf_ref, zi_ref, zg_ref, s0_ref, lbl_ref, ng_ref, o_ref, s_ref, *, layer):
    t_new = zq_ref.shape[1]
    lb = _lower_bound(lbl_ref[...], layer)
    q_all = _silu(zq_ref[0])
    f_all = lb + (1.0 - lb) * _sigmoid(zf_ref[0])
    v_all = zi_ref[0]
    zg_all = zg_ref[0]
    ng = ng_ref[...]
    tpos = lax.broadcasted_iota(jnp.int32, (t_new, DK_HG), 0)
    pad_rows = jnp.zeros((8 - t_new, DK_HG), F32)
    outs = []
    for h in range(H_HG):
        sl = slice(h * DK_HG, (h + 1) * DK_HG)
        q, f, v = q_all[:, sl], f_all[:, sl], v_all[:, sl]
        k = 1.0 - f
        logf = jnp.log(f)
        rows = [logf[0:1, :]]
        for t in range(1, t_new):
            rows.append(rows[-1] + logf[t:t + 1, :])
        g = jnp.concatenate(rows, axis=0)
        g_last = rows[-1]
        s0 = s0_ref[0, h]
        qe = jnp.concatenate([q * jnp.exp(g), pad_rows], axis=0)
        o = _bdot(qe, s0)[0:t_new]
        for s in range(t_new):
            d = jnp.exp(jnp.where(tpos >= s, g - g[s:s + 1, :], 0.0))
            w = jnp.where(tpos >= s, q * k[s:s + 1, :] * d, 0.0)
            o = o + jnp.sum(w, axis=-1, keepdims=True) * v[s:s + 1, :]
        kd = k * jnp.exp(g_last - g)
        cols = jnp.concatenate([jnp.exp(g_last), kd, jnp.zeros((DK_HG - 1 - t_new, DK_HG), F32)], axis=0).T
        s_new = s0 * cols[:, 0:1]
        for s in range(t_new):
            s_new = s_new + cols[:, 1 + s:2 + s] * v[s:s + 1, :]
        s_ref[0, h] = s_new
        o = o * lax.rsqrt(jnp.mean(o * o, axis=-1, keepdims=True) + LN_EPS) * ng[:, sl]
        outs.append(o * _silu(zg_all[:, sl]))
    o_ref[0] = jnp.concatenate(outs, axis=1).astype(o_ref.dtype)


def hgrn_step(z3, s0, lb_logits, norm_g, layer):
    bsz, t_new = z3.shape[:2]
    w = H_HG * DK_HG
    col = lambda cidx: pl.BlockSpec((1, t_new, w), lambda b, o=(OFF_HG + cidx * w) // w: (b, 0, o))
    assert OFF_HG % w == 0
    st = pl.BlockSpec((1, H_HG, DK_HG, DV_HG), lambda b: (b, 0, 0, 0))
    return pl.pallas_call(
        functools.partial(_hgrn_step_kernel, layer=layer),
        grid=(bsz,),
        in_specs=[col(0), col(1), col(2), col(3), st,
                  pl.BlockSpec((DEPTH, w), lambda b: (0, 0)), pl.BlockSpec((1, w), lambda b: (0, 0))],
        out_specs=[pl.BlockSpec((1, t_new, w), lambda b: (b, 0, 0)), st],
        out_shape=[jax.ShapeDtypeStruct((bsz, t_new, w), BF16), jax.ShapeDtypeStruct(s0.shape, F32)],
        compiler_params=_params(("parallel",)),
        name="hgrn_step",
    )(z3, z3, z3, z3, s0, lb_logits, norm_g)


CONV_HALO = 32


def _glu_ln_silu(y, g_ref, b_ref):
    mu = jnp.mean(y, axis=-1, keepdims=True)
    d = y - mu
    var = jnp.mean(d * d, axis=-1, keepdims=True)
    return _silu(d * lax.rsqrt(var + LN_EPS) * g_ref[...] + b_ref[...])


def _conv_kernel(a_ref, g_ref, w_ref, b_ref, lg_ref, lb_ref, o_ref, tail_ref, buf_s):
    i = pl.program_id(0)
    tb = a_ref.shape[0]

    @pl.when(i == 0)
    def _():
        buf_s[0:CONV_HALO, :] = jnp.zeros((CONV_HALO, D_CONV), F32)

    buf_s[CONV_HALO:, :] = a_ref[...] * _sigmoid(g_ref[...])
    off = CONV_HALO - (CONV_W - 1)
    y = jnp.zeros((tb, D_CONV), F32) + b_ref[...]
    for k in range(CONV_W):
        y = y + buf_s[off + k:off + k + tb, :] * w_ref[k:k + 1, :]
    o_ref[...] = _glu_ln_silu(y, lg_ref, lb_ref).astype(o_ref.dtype)
    tail = buf_s[tb:tb + CONV_HALO, :]
    tail_ref[...] = tail
    buf_s[0:CONV_HALO, :] = tail


def conv_prompt(z, w_dw, b_dw, ln_g, ln_b, tb=512):
    t = z.shape[0]
    tb = min(tb, t)
    vec = pl.BlockSpec((1, D_CONV), lambda i: (0, 0))
    o, tail = pl.pallas_call(
        _conv_kernel,
        grid=(t // tb,),
        in_specs=[pl.BlockSpec((tb, D_CONV), lambda i: (i, OFF_CONV // D_CONV)),
                  pl.BlockSpec((tb, D_CONV), lambda i: (i, OFF_CONV // D_CONV + 1)),
                  pl.BlockSpec((CONV_W, D_CONV), lambda i: (0, 0)), vec, vec, vec],
        out_specs=[pl.BlockSpec((tb, D_CONV), lambda i: (i, 0)), pl.BlockSpec((CONV_HALO, D_CONV), lambda i: (0, 0))],
        out_shape=[jax.ShapeDtypeStruct((t, D_CONV), BF16), jax.ShapeDtypeStruct((CONV_HALO, D_CONV), F32)],
        scratch_shapes=[pltpu.VMEM((tb + CONV_HALO, D_CONV), F32)],
        compiler_params=_params(("arbitrary",)),
        name="conv_prompt",
    )(z, z, w_dw, b_dw[None], ln_g[None], ln_b[None])
    return o, tail[CONV_HALO - (CONV_W - 1):]


def _conv_step_kernel(a_ref, g_ref, st_ref, wsh_ref, wnew_ref, b_ref, lg_ref, lb_ref, o_ref, ns_ref):
    t_new = a_ref.shape[1]
    hist = CONV_W - 1
    u = [a_ref[:, s, :] * _sigmoid(g_ref[:, s, :]) for s in range(t_new)]
    st = st_ref[...]
    for t in range(t_new):
        y = jnp.sum(st * wsh_ref[t], axis=1) + b_ref[...]
        for s in range(t + 1):
            y = y + u[s] * wnew_ref[t, s:s + 1, :]
        o_ref[:, t, :] = _glu_ln_silu(y, lg_ref, lb_ref).astype(o_ref.dtype)
    ns_ref[:, 0:hist - t_new, :] = st_ref[:, t_new:, :]
    for s in range(t_new):
        ns_ref[:, hist - t_new + s, :] = u[s]


def conv_step(z3, state, w_dw, b_dw, ln_g, ln_b, bb=8):
    bsz, t_new = z3.shape[:2]
    hist = CONV_W - 1
    wsh = jnp.stack([jnp.concatenate([jnp.zeros((t, D_CONV), F32), w_dw[:hist - t]], axis=0) for t in range(t_new)])
    wnew = jnp.stack([jnp.concatenate([w_dw[hist - t:], jnp.zeros((t_new - 1 - t, D_CONV), F32)], axis=0)
                      for t in range(t_new)])
    vec = pl.BlockSpec((1, D_CONV), lambda i: (0, 0))
    return pl.pallas_call(
        _conv_step_kernel,
        grid=(bsz // bb,),
        in_specs=[pl.BlockSpec((bb, t_new, D_CONV), lambda i: (i, 0, OFF_CONV // D_CONV)),
                  pl.BlockSpec((bb, t_new, D_CONV), lambda i: (i, 0, OFF_CONV // D_CONV + 1)),
                  pl.BlockSpec((bb, hist, D_CONV), lambda i: (i, 0, 0)),
                  pl.BlockSpec((t_new, hist, D_CONV), lambda i: (0, 0, 0)),
                  pl.BlockSpec((t_new, t_new, D_CONV), lambda i: (0, 0, 0)), vec, vec, vec],
        out_specs=[pl.BlockSpec((bb, t_new, D_CONV), lambda i: (i, 0, 0)),
                   pl.BlockSpec((bb, hist, D_CONV), lambda i: (i, 0, 0))],
        out_shape=[jax.ShapeDtypeStruct((bsz, t_new, D_CONV), BF16), jax.ShapeDtypeStruct(state.shape, F32)],
        compiler_params=_params(("parallel",)),
        name="conv_step",
    )(z3, z3, state, wsh, wnew, b_dw[None], ln_g[None], ln_b[None])


def _relayout_w_in(w):
    o_conv, o_q, o_kv, o_ng = 0, 2048, 3072, 4608
    o_h, o_mg, end = 4656, 8752, 14896
    parts = [w[:, o_mg:end], w[:, o_conv:o_q], w[:, o_q:o_kv], w[:, o_h:o_mg], w[:, o_kv:o_ng], w[:, o_ng:o_h],
             jnp.zeros((w.shape[0], NZ - OFF_NG - 3 * N_HEADS), w.dtype)]
    return jnp.concatenate(parts, axis=1).astype(BF16)


def _route_weights(wg, bg, we, be):
    pad = ROUTE_W - N_EXPERTS - N_GROUPS
    w = jnp.concatenate([we, wg, jnp.zeros((we.shape[0], pad), F32)], axis=1)
    b = jnp.concatenate([be, bg, jnp.zeros((pad,), F32)])[None, :]
    return w, b


def _layer(x, mods, P, l, conv_prev, s0, batch, nsa_state):
    m = x.shape[0]
    t = m // batch
    (sh1, sc1, g1), (sh2, sc2, g2) = mods
    u = modulate(x, sc1, sh1)
    z = matmul(u, P['w_in_r'][l], tn=NZ_TILE)
    z3 = z.reshape(batch, t, NZ)
    conv_w = (P['w_dw'][l], P['b_dw'][l], P['conv_ln_g'][l], P['conv_ln_b'][l])
    cmp_w = (P['w_cmp_pe'][l], P['w_cmp_1'][l], P['b_cmp_1'][l], P['w_cmp_2'][l])
    if nsa_state is None:
        a_conv, conv_new = conv_prompt(z, *conv_w)
        conv_new = conv_new[None]
        o_hg, s_new = hgrn_prompt(z, P['hgrn_lb_logits'], P['hgrn_norm_g'][l][None], l)
        s_new = s_new[None]
        o_nsa, kv6 = nsa_prompt(z, *cmp_w)
        kv6 = kv6[None]
        win_new = kv6[:, -min(WINDOW, t):, :, 4:6]
    else:
        a_conv, conv_new = conv_step(z3, conv_prev, *conv_w)
        o_hg, s_new = hgrn_step(z3, s0, P['hgrn_lb_logits'], P['hgrn_norm_g'][l][None], l)
        pool_cmp, pool_t, win_t, page_table, win_buf = nsa_state
        o_nsa, kv6 = nsa_sample(z, pool_cmp, pool_t, win_t, l, page_table, *cmp_w)
        win_new = jnp.concatenate([win_buf, kv6[..., 4:6, :]], axis=1)[:, -win_buf.shape[1]:]
    merged = merge_branches(a_conv.reshape(m, D_CONV), o_nsa, o_hg.reshape(m, -1),
                            P['w_conv_out'][l], P['w_proj_nsa'][l], P['w_proj_hgrn'][l], z)
    y = matmul(merged, P['w_out'][l], tn=512)
    w_r, b_r = _route_weights(P['w_route_group'][l], P['b_route_group'][l], P['w_route_expert'][l], P['b_route_expert'][l])
    x1, u2, comb = ln_residual_route(x, y, g1, P['ln_g'][l, 0][None], P['ln_b'][l, 0][None], sc2, sh2, w_r, b_r)
    y2 = moe_experts(u2, comb, P['w_exp_gate'][l], P['w_exp_up'][l], P['w_exp_down'][l])
    x2 = ln_residual(x1, y2, g2, P['ln_g'][l, 1][None], P['ln_b'][l, 1][None])
    return x2, (kv6[..., 0:4, :], win_new, conv_new, s_new)


def kernel(x_prompt, x_sample, cache_nsa_kv, state_win_kv, state_conv, state_hgrn, page_table,
           c_prompt, c_sample, w_ada, b_ada, w_in, w_cmp_pe, w_cmp_1, b_cmp_1, w_cmp_2,
           w_dw, b_dw, conv_ln_g, conv_ln_b, w_conv_out, hgrn_lb_logits, hgrn_norm_g,
           w_proj_nsa, w_proj_hgrn, w_out, ln_g, ln_b, w_route_group, b_route_group,
           w_route_expert, b_route_expert, w_exp_gate, w_exp_up, w_exp_down):
    P = dict(w_cmp_pe=w_cmp_pe, w_cmp_1=w_cmp_1, b_cmp_1=b_cmp_1, w_cmp_2=w_cmp_2, w_dw=w_dw, b_dw=b_dw,
             conv_ln_g=conv_ln_g, conv_ln_b=conv_ln_b, w_conv_out=w_conv_out, hgrn_norm_g=hgrn_norm_g,
             w_proj_nsa=w_proj_nsa, w_proj_hgrn=w_proj_hgrn, w_out=w_out, ln_g=ln_g, ln_b=ln_b,
             w_route_group=w_route_group, b_route_group=b_route_group, w_route_expert=w_route_expert,
             b_route_expert=b_route_expert, w_exp_gate=w_exp_gate, w_exp_up=w_exp_up, w_exp_down=w_exp_down)
    P['w_in_r'] = [_relayout_w_in(w_in[l]) for l in range(DEPTH)]
    P['hgrn_lb_logits'] = hgrn_lb_logits

    bp, tp = x_prompt.shape[:2]
    bs, ts = x_sample.shape[:2]
    c_all = jnp.concatenate([c_sample, c_prompt, jnp.zeros((8 - bp % 8, D_MODEL), F32)], axis=0)
    ada = ada_all(c_all, w_ada, b_ada)

    def mods(l, i, rows, rep):
        mrow = ada[2 * l + i, rows]
        if rep > 1:
            mrow = jnp.repeat(mrow, rep, axis=0)
        return mrow[:, :D_MODEL], mrow[:, D_MODEL:2 * D_MODEL], mrow[:, 2 * D_MODEL:]

    n_pool = cache_nsa_kv.shape[1]
    per_page = PAGE_SIZE // D_CMP
    pool_cmp = (cache_nsa_kv[:, :, :, :, 0:2].reshape(DEPTH, n_pool, per_page, D_CMP, N_KV, 2, HEAD_DIM)
                .transpose(0, 1, 4, 5, 2, 3, 6).reshape(DEPTH, n_pool, 2 * N_KV, per_page, D_CMP * HEAD_DIM)
                .astype(BF16))
    pool_t = cache_nsa_kv.transpose(0, 1, 3, 4, 5, 2)
    win_t = state_win_kv.transpose(0, 1, 3, 4, 5, 2)

    xp = x_prompt.reshape(bp * tp, D_MODEL)
    xs = x_sample.reshape(bs * ts, D_MODEL)
    outs_p, outs_s = [], []
    for l in range(DEPTH):
        mp = [mods(l, i, slice(bs, bs + bp), 1) for i in range(2)]
        xp, st = _layer(xp, mp, P, l, None, None, bp, None)
        outs_p.append(st)
        ms = [mods(l, i, slice(0, bs), ts) for i in range(2)]
        xs, st = _layer(xs, ms, P, l, state_conv[l], state_hgrn[l], bs,
                        (pool_cmp, pool_t, win_t, page_table, state_win_kv[l]))
        outs_s.append(st)
    stack = lambda outs, i: jnp.stack([o[i] for o in outs])
    return (xp.reshape(bp, tp, D_MODEL), xs.reshape(bs, ts, D_MODEL),
            stack(outs_p, 0), stack(outs_p, 1), stack(outs_p, 2), stack(outs_p, 3),
            stack(outs_s, 0), stack(outs_s, 1), stack(outs_s, 2), stack(outs_s, 3))
```

```python
import functools

import numpy as np
import jax
import jax.numpy as jnp
from jax import lax
from jax.experimental import pallas as pl
from jax.experimental.pallas import tpu as pltpu

F32 = jnp.float32
BF16 = jnp.bfloat16

D_MODEL = 2048
DEPTH = 2
D_CONV = 1024
CONV_W = 31
N_HEADS = 16
N_KV = 4
HEAD_DIM = 64
GRP = N_HEADS // N_KV
L_CMP = 32
D_CMP = 16
CMP_HID = 256
BLK_SLC = 64
TOP_N = 16
WINDOW = 512
H_HG = 8
DK_HG = 128
DV_HG = 128
HG_CHUNK = 64
N_GROUPS = 4
EXP_PER_GROUP = 8
N_EXPERTS = N_GROUPS * EXP_PER_GROUP
D_EXPERT = 256
PAGE_SIZE = 128

ALPHA = (2 * DEPTH) ** 0.25
LN_EPS = 1e-5
NEG = -1e30
BIG = 1e9

OFF_MG = 0
OFF_CONV = 3 * D_MODEL
OFF_Q = OFF_CONV + 2 * D_CONV
OFF_HG = OFF_Q + N_HEADS * HEAD_DIM
OFF_KV = OFF_HG + 4 * H_HG * DK_HG
OFF_NG = OFF_KV + N_KV * 6 * HEAD_DIM
NZ = OFF_NG + 128
NZ_TILE = 1152
ROUTE_W = 128


def _params(sem, vmem_mb=48):
    return pltpu.CompilerParams(dimension_semantics=sem, vmem_limit_bytes=vmem_mb * 1024 * 1024)


def _sigmoid(x):
    return 1.0 / (1.0 + jnp.exp(-x))


def _silu(x):
    return x * _sigmoid(x)


def _bdot(a, b):
    return jnp.dot(a.astype(BF16), b.astype(BF16), preferred_element_type=F32)


def _row_spec(arr, tm):
    d = arr.shape[1]
    if arr.shape[0] == 1:
        return pl.BlockSpec((1, d), lambda i: (0, 0))
    return pl.BlockSpec((tm, d), lambda i: (i, 0))


def _mm_kernel(x_ref, w_ref, o_ref):
    o_ref[...] = _bdot(x_ref[...], w_ref[...]).astype(o_ref.dtype)


def matmul(x, w, *, tn, out_dtype=F32, tm=1024):
    m, k = x.shape
    n = w.shape[1]
    tm = min(tm, m)
    return pl.pallas_call(
        _mm_kernel,
        grid=(m // tm, n // tn),
        in_specs=[pl.BlockSpec((tm, k), lambda i, j: (i, 0)),
                  pl.BlockSpec((k, tn), lambda i, j: (0, j))],
        out_specs=pl.BlockSpec((tm, tn), lambda i, j: (i, j)),
        out_shape=jax.ShapeDtypeStruct((m, n), out_dtype),
        compiler_params=_params(("parallel", "arbitrary")),
        name="matmul",
    )(x, w)


def _ada_kernel(c_ref, w_ref, b_ref, o_ref, *, tn):
    j = pl.program_id(1)
    m = _bdot(_silu(c_ref[...]), w_ref[0]) + b_ref[0]
    col = j * tn + lax.broadcasted_iota(jnp.int32, m.shape, 1)
    o_ref[0] = m + (col >= D_MODEL).astype(F32)


def ada_all(c, w_ada, b_ada, tn=1024):
    r = c.shape[0]
    n = 3 * D_MODEL
    w = w_ada.reshape(2 * DEPTH, D_MODEL, n)
    b = b_ada.reshape(2 * DEPTH, 1, n)
    return pl.pallas_call(
        functools.partial(_ada_kernel, tn=tn),
        grid=(2 * DEPTH, n // tn),
        in_specs=[pl.BlockSpec((r, D_MODEL), lambda a, j: (0, 0)),
                  pl.BlockSpec((1, D_MODEL, tn), lambda a, j: (a, 0, j)),
                  pl.BlockSpec((1, 1, tn), lambda a, j: (a, 0, j))],
        out_specs=pl.BlockSpec((1, r, tn), lambda a, j: (a, 0, j)),
        out_shape=jax.ShapeDtypeStruct((2 * DEPTH, r, n), F32),
        compiler_params=_params(("parallel", "arbitrary")),
        name="ada",
    )(c, w, b)


def _mod_kernel(x_ref, sc_ref, sh_ref, o_ref):
    o_ref[...] = (x_ref[...] * sc_ref[...] + sh_ref[...]).astype(o_ref.dtype)


def modulate(x, scale, shift, tm=512):
    m, d = x.shape
    tm = min(tm, m)
    return pl.pallas_call(
        _mod_kernel,
        grid=(m // tm,),
        in_specs=[pl.BlockSpec((tm, d), lambda i: (i, 0)), _row_spec(scale, tm), _row_spec(shift, tm)],
        out_specs=pl.BlockSpec((tm, d), lambda i: (i, 0)),
        out_shape=jax.ShapeDtypeStruct((m, d), BF16),
        compiler_params=_params(("parallel",)),
        name="modulate",
    )(x, scale, shift)


def _post_ln(x_ref, y_ref, gate_ref, g_ref, b_ref):
    v = ALPHA * x_ref[...] + gate_ref[...] * y_ref[...]
    mu = jnp.mean(v, axis=-1, keepdims=True)
    d = v - mu
    var = jnp.mean(d * d, axis=-1, keepdims=True)
    return d * lax.rsqrt(var + LN_EPS) * g_ref[...] + b_ref[...]


def _ln_kernel(x_ref, y_ref, gate_ref, g_ref, b_ref, o_ref):
    o_ref[...] = _post_ln(x_ref, y_ref, gate_ref, g_ref, b_ref)


def _split_bf16(a):
    hi = a.astype(BF16)
    lo = (a - hi.astype(F32)).astype(BF16)
    return hi, lo


def _route(logits):
    lane = lax.broadcasted_iota(jnp.int32, logits.shape, 1).astype(F32)
    far = jnp.float32(1e9)
    is_g = (lane >= N_EXPERTS) & (lane < N_EXPERTS + N_GROUPS)
    gl = jnp.where(is_g, logits, -jnp.inf)
    gmax = jnp.max(gl, axis=1, keepdims=True)
    gidx = jnp.min(jnp.where(gl == gmax, lane, far), axis=1, keepdims=True) - N_EXPERTS
    wg = 1.0 / jnp.sum(jnp.where(is_g, jnp.exp(gl - gmax), 0.0), axis=1, keepdims=True)
    lo = gidx * EXP_PER_GROUP
    in_g = (lane >= lo) & (lane < lo + EXP_PER_GROUP)
    el = jnp.where(in_g, logits, -jnp.inf)
    m1 = jnp.max(el, axis=1, keepdims=True)
    i1 = jnp.min(jnp.where(el == m1, lane, far), axis=1, keepdims=True)
    el2 = jnp.where(lane == i1, -jnp.inf, el)
    m2 = jnp.max(el2, axis=1, keepdims=True)
    i2 = jnp.min(jnp.where(el2 == m2, lane, far), axis=1, keepdims=True)
    e2 = jnp.exp(m2 - m1)
    t1 = 1.0 / (1.0 + e2)
    t2 = e2 / (1.0 + e2)
    return jnp.where(lane == i1, wg * t1, 0.0) + jnp.where(lane == i2, wg * t2, 0.0)


def _ln_route_kernel(x_ref, y_ref, gate_ref, g_ref, b_ref, sc_ref, sh_ref, wr_ref, br_ref,
                     xn_ref, u_ref, comb_ref):
    xn = _post_ln(x_ref, y_ref, gate_ref, g_ref, b_ref)
    xn_ref[...] = xn
    u = xn * sc_ref[...] + sh_ref[...]
    u_ref[...] = u.astype(BF16)
    u_hi, u_lo = _split_bf16(u)
    w_hi, w_lo = _split_bf16(wr_ref[...])
    logits = (jnp.dot(u_hi, w_hi, preferred_element_type=F32)
              + jnp.dot(u_lo, w_hi, preferred_element_type=F32)
              + jnp.dot(u_hi, w_lo, preferred_element_type=F32)) + br_ref[...]
    comb_ref[...] = _route(logits)


def ln_residual(x, y, gate, g, b, tm=512):
    m, d = x.shape
    tm = min(tm, m)
    row = pl.BlockSpec((tm, d), lambda i: (i, 0))
    vec = pl.BlockSpec((1, d), lambda i: (0, 0))
    return pl.pallas_call(
        _ln_kernel,
        grid=(m // tm,),
        in_specs=[row, row, _row_spec(gate, tm), vec, vec],
        out_specs=row,
        out_shape=jax.ShapeDtypeStruct((m, d), F32),
        compiler_params=_params(("parallel",)),
        name="ln_residual",
    )(x, y, gate, g, b)


def ln_residual_route(x, y, gate, g, b, scale, shift, w_route, b_route, tm=512):
    m, d = x.shape
    tm = min(tm, m)
    row = pl.BlockSpec((tm, d), lambda i: (i, 0))
    vec = pl.BlockSpec((1, d), lambda i: (0, 0))
    return pl.pallas_call(
        _ln_route_kernel,
        grid=(m // tm,),
        in_specs=[row, row, _row_spec(gate, tm), vec, vec, _row_spec(scale, tm), _row_spec(shift, tm),
                  pl.BlockSpec((d, ROUTE_W), lambda i: (0, 0)), pl.BlockSpec((1, ROUTE_W), lambda i: (0, 0))],
        out_specs=[row, row, pl.BlockSpec((tm, ROUTE_W), lambda i: (i, 0))],
        out_shape=[jax.ShapeDtypeStruct((m, d), F32), jax.ShapeDtypeStruct((m, d), BF16),
                   jax.ShapeDtypeStruct((m, ROUTE_W), F32)],
        compiler_params=_params(("parallel",)),
        name="ln_residual_route",
    )(x, y, gate, g, b, scale, shift, w_route, b_route)


def _merge_kernel(ac_ref, on_ref, oh_ref, wc_ref, wn_ref, wh_ref, za_ref, zb_ref, zc_ref, o_ref):
    yc = _bdot(ac_ref[...], wc_ref[...])
    yn = _bdot(on_ref[...], wn_ref[...])
    yh = _bdot(oh_ref[...], wh_ref[...])
    o = _sigmoid(za_ref[...]) * yc + _sigmoid(zb_ref[...]) * yn + _sigmoid(zc_ref[...]) * yh
    o_ref[...] = o.astype(o_ref.dtype)


def merge_branches(a_conv, o_nsa, o_hg, w_conv_out, w_proj_nsa, w_proj_hgrn, z, tm=512, tn=512):
    m, k = a_conv.shape
    tm = min(tm, m)
    nb = D_MODEL // tn
    act = pl.BlockSpec((tm, k), lambda i, j: (i, 0))
    wsp = pl.BlockSpec((k, tn), lambda i, j: (0, j))
    gate = [pl.BlockSpec((tm, tn), lambda i, j, o=(OFF_MG + c * D_MODEL) // tn: (i, o + j)) for c in range(3)]
    return pl.pallas_call(
        _merge_kernel,
        grid=(m // tm, nb),
        in_specs=[act, act, act, wsp, wsp, wsp] + gate,
        out_specs=pl.BlockSpec((tm, tn), lambda i, j: (i, j)),
        out_shape=jax.ShapeDtypeStruct((m, D_MODEL), BF16),
        compiler_params=_params(("parallel", "arbitrary")),
        name="merge_branches",
    )(a_conv, o_nsa, o_hg, w_conv_out, w_proj_nsa, w_proj_hgrn, z, z, z)


def _moe_kernel(u_ref, comb_ref, wg_ref, wu_ref, wd_ref, o_ref):
    e = pl.program_id(1)

    @pl.when(e == 0)
    def _():
        o_ref[...] = jnp.zeros_like(o_ref)

    u = u_ref[...]
    a = _bdot(u, wg_ref[0])
    b = _bdot(u, wu_ref[0])
    comb = comb_ref[...]
    lane = lax.broadcasted_iota(jnp.int32, comb.shape, 1)
    c = jnp.sum(jnp.where(lane == e, comb, 0.0), axis=1, keepdims=True)
    h = _silu(a) * b * c
    o_ref[...] += _bdot(h, wd_ref[0])


def moe_experts(u, comb, w_gate, w_up, w_down, tm=1024):
    m, d = u.shape
    tm = min(tm, m)
    return pl.pallas_call(
        _moe_kernel,
        grid=(m // tm, N_EXPERTS),
        in_specs=[pl.BlockSpec((tm, d), lambda i, e: (i, 0)),
                  pl.BlockSpec((tm, ROUTE_W), lambda i, e: (i, 0)),
                  pl.BlockSpec((1, d, D_EXPERT), lambda i, e: (e, 0, 0)),
                  pl.BlockSpec((1, d, D_EXPERT), lambda i, e: (e, 0, 0)),
                  pl.BlockSpec((1, D_EXPERT, d), lambda i, e: (e, 0, 0))],
        out_specs=pl.BlockSpec((tm, d), lambda i, e: (i, 0)),
        out_shape=jax.ShapeDtypeStruct((m, d), F32),
        compiler_params=_params(("parallel", "arbitrary")),
        name="moe_experts",
    )(u, comb, w_gate, w_up, w_down)


def _compress_kernel(x_ref, pa_ref, pb_ref, wa_ref, wb_ref, b1_ref, w2_ref, o_ref):
    x = x_ref[0]
    rows = x.shape[0]
    ha = _bdot(x + pa_ref[0], wa_ref[0])
    hb = _bdot(x + pb_ref[0], wb_ref[0])
    hb_next = pltpu.roll(hb, rows - 1, 0)
    h = _silu(ha + hb_next + b1_ref[0])
    o_ref[0] = _bdot(h, w2_ref[0])


def nsa_compress_blocks(x, pe, w1, b1, w2):
    rows = x.shape[1]
    kd = D_CMP * HEAD_DIM
    pa = pe[:, :D_CMP].reshape(2, 1, kd)
    pb = pe[:, D_CMP:].reshape(2, 1, kd)
    wa = w1[:, :D_CMP].reshape(2, kd, CMP_HID)
    wb = w1[:, D_CMP:].reshape(2, kd, CMP_HID)
    s_of = lambda i: (i % 2, 0, 0)
    return pl.pallas_call(
        _compress_kernel,
        grid=(2 * N_KV,),
        in_specs=[pl.BlockSpec((1, rows, kd), lambda i: (i, 0, 0)),
                  pl.BlockSpec((1, 1, kd), s_of), pl.BlockSpec((1, 1, kd), s_of),
                  pl.BlockSpec((1, kd, CMP_HID), s_of), pl.BlockSpec((1, kd, CMP_HID), s_of),
                  pl.BlockSpec((1, 1, CMP_HID), s_of), pl.BlockSpec((1, CMP_HID, HEAD_DIM), s_of)],
        out_specs=pl.BlockSpec((1, rows, HEAD_DIM), lambda i: (i, 0, 0)),
        out_shape=jax.ShapeDtypeStruct((2 * N_KV, rows, HEAD_DIM), F32),
        compiler_params=_params(("parallel",)),
        name="nsa_compress",
    )(x, pa, pb, wa, wb, b1.reshape(2, 1, CMP_HID), w2)


def _cmp_kernel(qT_ref, kc_ref, vcT_ref, ov_ref, o_ref, bias_ref, *, tq, k_sel):
    qb = pl.program_id(1)
    kc = kc_ref[0]
    vcT = vcT_ref[0]
    nblk = kc.shape[0]
    n_slc = ov_ref.shape[0]
    tpos = qb * tq + lax.broadcasted_iota(jnp.int32, (nblk, tq), 1)
    last = lax.broadcasted_iota(jnp.int32, (nblk, tq), 0) * D_CMP + (L_CMP - 1)
    vis = last <= tpos
    visf = vis.astype(F32)
    psum = jnp.zeros((nblk, tq), F32)
    for r in range(GRP):
        s = jnp.dot(kc, qT_ref[r], preferred_element_type=F32)
        s = jnp.where(vis, s, NEG)
        e = jnp.exp(s - jnp.max(s, axis=0, keepdims=True))
        p = e / jnp.sum(e, axis=0, keepdims=True) * visf
        o_ref[r] = jnp.dot(vcT, p.astype(BF16), preferred_element_type=F32)
        psum = psum + p
    p_hi, p_lo = _split_bf16(psum)
    ov = ov_ref[...]
    imp = jnp.dot(ov, p_hi, preferred_element_type=F32) + jnp.dot(ov, p_lo, preferred_element_type=F32)
    j = lax.broadcasted_iota(jnp.int32, (n_slc, tq), 0)
    cur = (qb * tq + lax.broadcasted_iota(jnp.int32, (n_slc, tq), 1)) // BLK_SLC
    valid = j <= cur
    forced = (j == 0) | (j == cur) | (j == cur - 1)
    score = jnp.where(forced, BIG, jnp.where(valid, imp, -BIG))
    jf = j.astype(F32)
    sel = jnp.zeros((n_slc, tq), F32)
    for _ in range(k_sel):
        m = jnp.max(score, axis=0, keepdims=True)
        idx = jnp.min(jnp.where(score == m, jf, 1e9), axis=0, keepdims=True)
        hit = jf == idx
        sel = jnp.where(hit, 1.0, sel)
        score = jnp.where(hit, -3e38, score)
    bias_ref[0] = jnp.where((sel > 0.0) & valid, 0.0, NEG).astype(BF16)


def _slc_kernel(qT_ref, bias_ref, k_ref, vT_ref, o_ref, rhs_s, m_s, l_s, acc_s, *, tq):
    qb = pl.program_id(1)
    m_s[...] = jnp.full(m_s.shape, NEG, F32)
    l_s[...] = jnp.zeros(l_s.shape, F32)
    acc_s[...] = jnp.zeros(acc_s.shape, F32)
    for r in range(GRP):
        rhs_s[0:HEAD_DIM, r * tq:(r + 1) * tq] = qT_ref[r]
        rhs_s[HEAD_DIM:, r * tq:(r + 1) * tq] = bias_ref[0]

    def tile(kt, diagonal):
        s = jnp.dot(k_ref[0, kt], rhs_s[...], preferred_element_type=F32)
        if diagonal:
            kpos = kt * tq + lax.broadcasted_iota(jnp.int32, (tq, GRP * tq), 0)
            tpos = qb * tq + lax.broadcasted_iota(jnp.int32, (tq, GRP * tq), 1) % tq
            s = jnp.where(kpos <= tpos, s, NEG)
        m_old = m_s[...]
        m_new = jnp.maximum(m_old, jnp.max(s, axis=0, keepdims=True))
        alpha = jnp.exp(m_old - m_new)
        p = jnp.exp(s - m_new)
        l_s[...] = alpha * l_s[...] + jnp.sum(p, axis=0, keepdims=True)
        acc_s[...] = alpha * acc_s[...] + jnp.dot(vT_ref[0, kt], p.astype(BF16), preferred_element_type=F32)
        m_s[...] = m_new

    def body(kt, carry):
        tile(kt, False)
        return carry

    lax.fori_loop(0, qb, body, 0)
    tile(qb, True)
    o = acc_s[...] / l_s[...]
    for r in range(GRP):
        o_ref[r] = o[:, r * tq:(r + 1) * tq]


def _win_kernel(qT_ref, k_ref, vT_ref, oc_ref, os_ref, zng_ref, o_ref, rhs_s, m_s, l_s, acc_s, *, tq):
    qb = pl.program_id(1)
    m_s[...] = jnp.full(m_s.shape, NEG, F32)
    l_s[...] = jnp.zeros(l_s.shape, F32)
    acc_s[...] = jnp.zeros(acc_s.shape, F32)
    for r in range(GRP):
        rhs_s[:, r * tq:(r + 1) * tq] = qT_ref[r]
    tpos = qb * tq + lax.broadcasted_iota(jnp.int32, (tq, GRP * tq), 1) % tq
    row = lax.broadcasted_iota(jnp.int32, (tq, GRP * tq), 0)
    for i in range(WINDOW // tq + 1):
        kt_raw = qb - i
        kt = jnp.maximum(kt_raw, 0)
        kpos = kt_raw * tq + row
        ok = (kpos <= tpos) & (kpos > tpos - WINDOW) & (kpos >= 0)
        s = jnp.where(ok, jnp.dot(k_ref[0, kt], rhs_s[...], preferred_element_type=F32), NEG)
        m_old = m_s[...]
        m_new = jnp.maximum(m_old, jnp.max(s, axis=0, keepdims=True))
        alpha = jnp.exp(m_old - m_new)
        p = jnp.exp(s - m_new)
        l_s[...] = alpha * l_s[...] + jnp.sum(p, axis=0, keepdims=True)
        acc_s[...] = alpha * acc_s[...] + jnp.dot(vT_ref[0, kt], p.astype(BF16), preferred_element_type=F32)
        m_s[...] = m_new
    o_win = acc_s[...] / l_s[...]
    zng = zng_ref[0]
    for r in range(GRP):
        g_cmp = _sigmoid(zng[r:r + 1, :])
        g_slc = _sigmoid(zng[GRP + r:GRP + r + 1, :])
        g_win = _sigmoid(zng[2 * GRP + r:2 * GRP + r + 1, :])
        o_ref[r] = (g_cmp * oc_ref[r] + g_slc * os_ref[r] + g_win * o_win[:, r * tq:(r + 1) * tq]).astype(o_ref.dtype)


def _tiles_rows(a, tq):
    g, t, c = a.shape
    return a.reshape(g, t // tq, tq, c)


def _tiles_cols(a, tq):
    g, c, t = a.shape
    return a.reshape(g, c, t // tq, tq).transpose(0, 2, 1, 3)


def nsa_prompt(z, pe, w1, b1, w2, tq=256):
    t = z.shape[0]
    nblk = t // D_CMP
    n_slc = t // BLK_SLC
    k_sel = min(TOP_N, n_slc)
    nq = t // tq
    zq = z[:, OFF_Q:OFF_Q + N_HEADS * HEAD_DIM]
    kv6 = z[:, OFF_KV:OFF_KV + N_KV * 6 * HEAD_DIM].reshape(t, N_KV, 6, HEAD_DIM)
    zng = z[:, OFF_NG:OFF_NG + 3 * N_HEADS]
    qT = (zq.reshape(t, N_HEADS, HEAD_DIM).transpose(1, 2, 0) * (HEAD_DIM ** -0.5)).astype(BF16)
    x = kv6[:, :, 0:2].reshape(nblk, D_CMP, N_KV, 2, HEAD_DIM).transpose(2, 3, 0, 1, 4)
    cmp = nsa_compress_blocks(x.reshape(2 * N_KV, nblk, D_CMP * HEAD_DIM), pe, w1, b1, w2)
    kc = cmp[0::2].astype(BF16)
    vcT = cmp[1::2].transpose(0, 2, 1).astype(BF16)
    i0 = np.arange(nblk)[None, :] * D_CMP
    j0 = np.arange(n_slc)[:, None] * BLK_SLC
    ovT = jnp.asarray(((i0 < j0 + BLK_SLC) & (i0 + L_CMP > j0)).astype(np.float32), dtype=BF16)

    head_blk = pl.BlockSpec((GRP, HEAD_DIM, tq), lambda g, i: (g, 0, i))
    scratch = [pltpu.VMEM((1, GRP * tq), F32), pltpu.VMEM((1, GRP * tq), F32), pltpu.VMEM((HEAD_DIM, GRP * tq), F32)]
    o_cmp, bias = pl.pallas_call(
        functools.partial(_cmp_kernel, tq=tq, k_sel=k_sel),
        grid=(N_KV, nq),
        in_specs=[head_blk,
                  pl.BlockSpec((1, nblk, HEAD_DIM), lambda g, i: (g, 0, 0)),
                  pl.BlockSpec((1, HEAD_DIM, nblk), lambda g, i: (g, 0, 0)),
                  pl.BlockSpec((n_slc, nblk), lambda g, i: (0, 0))],
        out_specs=[head_blk, pl.BlockSpec((1, n_slc, tq), lambda g, i: (g, 0, i))],
        out_shape=[jax.ShapeDtypeStruct((N_HEADS, HEAD_DIM, t), F32),
                   jax.ShapeDtypeStruct((N_KV, n_slc, t), BF16)],
        compiler_params=_params(("parallel", "arbitrary")),
        name="nsa_cmp_select",
    )(qT, kc, vcT, ovT)

    k_slc = kv6[:, :, 2].transpose(1, 0, 2)
    onehot = (jnp.arange(t)[:, None] // BLK_SLC == jnp.arange(n_slc)[None, :]).astype(F32)
    k_aug = jnp.concatenate([k_slc, jnp.broadcast_to(onehot, (N_KV, t, n_slc))], axis=-1).astype(BF16)
    v_slcT = kv6[:, :, 3].transpose(1, 2, 0).astype(BF16)
    ka = HEAD_DIM + n_slc
    o_slc = pl.pallas_call(
        functools.partial(_slc_kernel, tq=tq),
        grid=(N_KV, nq),
        in_specs=[head_blk,
                  pl.BlockSpec((1, n_slc, tq), lambda g, i: (g, 0, i)),
                  pl.BlockSpec((1, nq, tq, ka), lambda g, i: (g, 0, 0, 0)),
                  pl.BlockSpec((1, nq, HEAD_DIM, tq), lambda g, i: (g, 0, 0, 0))],
        out_specs=head_blk,
        out_shape=jax.ShapeDtypeStruct((N_HEADS, HEAD_DIM, t), F32),
        scratch_shapes=[pltpu.VMEM((ka, GRP * tq), BF16)] + scratch,
        compiler_params=_params(("parallel", "arbitrary")),
        name="nsa_selected",
    )(qT, bias, _tiles_rows(k_aug, tq), _tiles_cols(v_slcT, tq))

    k_win = kv6[:, :, 4].transpose(1, 0, 2).astype(BF16)
    v_winT = kv6[:, :, 5].transpose(1, 2, 0).astype(BF16)
    zngT = zng.reshape(t, N_KV, GRP, 3).transpose(1, 3, 2, 0).reshape(N_KV, 3 * GRP, t)
    oT = pl.pallas_call(
        functools.partial(_win_kernel, tq=tq),
        grid=(N_KV, nq),
        in_specs=[head_blk,
                  pl.BlockSpec((1, nq, tq, HEAD_DIM), lambda g, i: (g, 0, 0, 0)),
                  pl.BlockSpec((1, nq, HEAD_DIM, tq), lambda g, i: (g, 0, 0, 0)),
                  head_blk, head_blk,
                  pl.BlockSpec((1, 3 * GRP, tq), lambda g, i: (g, 0, i))],
        out_specs=head_blk,
        out_shape=jax.ShapeDtypeStruct((N_HEADS, HEAD_DIM, t), BF16),
        scratch_shapes=[pltpu.VMEM((HEAD_DIM, GRP * tq), BF16)] + scratch,
        compiler_params=_params(("parallel", "arbitrary")),
        name="nsa_window_combine",
    )(qT, _tiles_rows(k_win, tq), _tiles_cols(v_winT, tq), o_cmp, o_slc, zngT)
    o = oT.transpose(2, 0, 1).reshape(t, N_HEADS * HEAD_DIM)
    return o, kv6


def _nt_dot(a, b):
    return lax.dot_general(a.astype(BF16), b.astype(BF16), (((1,), (1,)), ((), ())), preferred_element_type=F32)


def _nsa_sample_kernel(pt_ref, q_ref, zng_ref, kns_ref, knw_ref, win_ref, *rest, n_pages, past, t_new):
    cmp_pages = rest[:n_pages]
    slc_pages = rest[n_pages:2 * n_pages]
    pa_ref, pb_ref, wa_ref, wb_ref, wab_ref, sel_ref, b1_ref, w2_ref, ov_ref, ex_ref, o_ref = rest[2 * n_pages:]
    del pt_ref
    rows = GRP * t_new
    nblk = n_pages * (PAGE_SIZE // D_CMP)
    n_cmp = nblk - 1
    n_slc = -(-(past + t_new) // BLK_SLC)
    k_sel = min(TOP_N, n_slc)
    wlen = win_ref.shape[-1]
    lanes = 128
    lane = lax.broadcasted_iota(jnp.int32, (rows, lanes), 1)
    tok = lax.broadcasted_iota(jnp.int32, (rows, lanes), 0) % t_new
    qpos = past + tok
    wlane = lax.broadcasted_iota(jnp.int32, (rows, wlen), 1)
    wtok = lax.broadcasted_iota(jnp.int32, (rows, wlen), 0) % t_new

    def pe_term(p_ref, w_ref, s):
        hi, lo = _split_bf16(jnp.broadcast_to(p_ref[s], (8, p_ref.shape[-1])))
        w = w_ref[s]
        return (jnp.dot(hi, w, preferred_element_type=F32) + jnp.dot(lo, w, preferred_element_type=F32))[0:1]

    sel = sel_ref[...]
    per_page = PAGE_SIZE // D_CMP
    cmp = []
    for s in range(2):
        r = [[_nt_dot(sel, cmp_pages[p][0, 0, g, s]) for p in range(n_pages)] for g in range(N_KV)]
        hab = jnp.zeros((N_KV * nblk, 2 * CMP_HID), F32)
        for j in range(D_CMP):
            xj = jnp.concatenate([r[g][p][j * per_page:(j + 1) * per_page] for g in range(N_KV)
                                  for p in range(n_pages)], axis=0)
            hab = hab + _bdot(xj, wab_ref[s, j])
        ha = hab[:, :CMP_HID] + pe_term(pa_ref, wa_ref, s)
        hb = hab[:, CMP_HID:] + pe_term(pb_ref, wb_ref, s)
        h = _silu(ha + pltpu.roll(hb, N_KV * nblk - 1, 0) + b1_ref[s])
        cmp.append(_bdot(h, w2_ref[s]))

    new_ok = (lane < t_new) & (lane <= tok)
    zeros_tail = jnp.zeros((lanes - kns_ref.shape[2], lanes), BF16)
    for g in range(N_KV):
        q = q_ref[0, g]
        kc = cmp[0][g * nblk:(g + 1) * nblk]
        vc = cmp[1][g * nblk:(g + 1) * nblk]
        vis = (lane * D_CMP + (L_CMP - 1) <= qpos) & (lane < n_cmp)
        s = jnp.where(vis, _nt_dot(q[:, :HEAD_DIM], kc), NEG)
        e = jnp.exp(s - jnp.max(s, axis=1, keepdims=True))
        p = e / jnp.sum(e, axis=1, keepdims=True) * vis.astype(F32)
        o_cmp = _bdot(p, vc)
        psum = p
        for r in range(1, GRP):
            psum = psum + pltpu.roll(p, r * t_new, 0)
        p_hi, p_lo = _split_bf16(psum)
        ov = ov_ref[...]
        imp = jnp.dot(p_hi, ov, preferred_element_type=F32) + jnp.dot(p_lo, ov, preferred_element_type=F32)
        cur = qpos // BLK_SLC
        valid = (lane <= cur) & (lane < n_slc)
        forced = (lane == 0) | (lane == cur) | (lane == cur - 1)
        score = jnp.where(lane < n_slc, jnp.where(forced, BIG, jnp.where(valid, imp, -BIG)), -3e38)
        rank = jnp.zeros((rows, lanes), F32)
        for sh in range(1, n_slc):
            lower = pltpu.roll(score, sh, 1)
            upper = pltpu.roll(score, lanes - sh, 1)
            rank = rank + (lower >= score).astype(F32) + (upper > score).astype(F32)
        sel = (rank < k_sel) & valid
        bias = _bdot(jnp.where(sel, 0.0, NEG), ex_ref[...])
        qh = q[:, :HEAD_DIM]
        kv_new = jnp.concatenate([kns_ref[0, g], zeros_tail], axis=0)
        sc = [_bdot(qh, slc_pages[p][0, 0, g, 0]) + bias[:, p * lanes:(p + 1) * lanes] for p in range(n_pages)]
        s_new = jnp.where(new_ok, _nt_dot(q, kv_new) + bias[:, n_pages * lanes:(n_pages + 1) * lanes], NEG)
        m = s_new.max(axis=1, keepdims=True)
        for x in sc:
            m = jnp.maximum(m, x.max(axis=1, keepdims=True))
        e = jnp.exp(s_new - m)
        den = jnp.sum(e, axis=1, keepdims=True)
        acc = _bdot(e, kv_new)[:, HEAD_DIM:]
        for p in range(n_pages):
            e = jnp.exp(sc[p] - m)
            den = den + jnp.sum(e, axis=1, keepdims=True)
            acc = acc + _nt_dot(e, slc_pages[p][0, 0, g, 1])
        o_slc = acc / den
        kw_new = jnp.concatenate([knw_ref[0, g], zeros_tail], axis=0)
        wpos = past - wlen + wlane
        wq = past + wtok
        ok = (wpos <= wq) & (wpos > wq - WINDOW) & (wpos >= 0)
        s_old = jnp.where(ok, _bdot(qh, win_ref[0, 0, g, 0]), NEG)
        s_new = jnp.where(new_ok, _nt_dot(q, kw_new), NEG)
        m = jnp.maximum(s_old.max(axis=1, keepdims=True), s_new.max(axis=1, keepdims=True))
        e_old = jnp.exp(s_old - m)
        e_new = jnp.exp(s_new - m)
        den = jnp.sum(e_old, axis=1, keepdims=True) + jnp.sum(e_new, axis=1, keepdims=True)
        o_win = (_nt_dot(e_old, win_ref[0, 0, g, 1]) + _bdot(e_new, kw_new)[:, HEAD_DIM:]) / den
        gates = _sigmoid(zng_ref[0, g])
        o_ref[0, g] = gates[:, 0:1] * o_cmp + gates[:, 1:2] * o_slc + gates[:, 2:3] * o_win


def nsa_sample(z, pool_t, win_t, layer, page_table, pe, w1, b1, w2):
    bsz, n_pages = page_table.shape
    t_new = z.shape[0] // bsz
    past = n_pages * PAGE_SIZE
    wlen = win_t.shape[-1]
    rows = GRP * t_new
    nblk = past // D_CMP
    n_slc = -(-(past + t_new) // BLK_SLC)
    lanes = 128
    assert nblk == lanes and n_slc <= lanes and wlen % lanes == 0 and t_new <= 8

    def per_group(a, width):
        return a.reshape(bsz, t_new, N_KV, GRP, width).transpose(0, 2, 3, 1, 4).reshape(bsz, N_KV, rows, width)

    zq = per_group(z[:, OFF_Q:OFF_Q + N_HEADS * HEAD_DIM], HEAD_DIM) * (HEAD_DIM ** -0.5)
    q_pad = jnp.pad(zq, ((0, 0), (0, 0), (0, 0), (0, lanes - HEAD_DIM))).astype(BF16)
    zng = jnp.pad(per_group(z[:, OFF_NG:OFF_NG + 3 * N_HEADS], 3), ((0, 0), (0, 0), (0, 0), (0, lanes - 3)))
    kv6 = z[:, OFF_KV:OFF_KV + N_KV * 6 * HEAD_DIM].reshape(bsz, t_new, N_KV, 6, HEAD_DIM)

    def new_rows(c0):
        a = kv6[:, :, :, c0:c0 + 2].transpose(0, 2, 1, 3, 4).reshape(bsz, N_KV, t_new, 2 * HEAD_DIM)
        return jnp.pad(a, ((0, 0), (0, 0), (0, 8 - t_new), (0, 0))).astype(BF16)

    per_page = PAGE_SIZE // D_CMP
    kd = D_CMP * HEAD_DIM
    pa = pe[:, :D_CMP].reshape(2, 1, kd)
    pb = pe[:, D_CMP:].reshape(2, 1, kd)
    wa = w1[:, :D_CMP].reshape(2, kd, CMP_HID).astype(BF16)
    wb = w1[:, D_CMP:].reshape(2, kd, CMP_HID).astype(BF16)
    wab = jnp.concatenate([w1[:, :D_CMP], w1[:, D_CMP:]], axis=-1).astype(BF16)
    tok = np.arange(PAGE_SIZE)
    sel = jnp.asarray(((tok % D_CMP) * per_page + tok // D_CMP)[None, :] == np.arange(PAGE_SIZE)[:, None], dtype=BF16)
    i0 = np.arange(lanes)[:, None] * D_CMP
    j0 = np.arange(lanes)[None, :] * BLK_SLC
    ov = ((i0 < j0 + BLK_SLC) & (i0 + L_CMP > j0) & (np.arange(lanes)[:, None] < nblk - 1)
          & (np.arange(lanes)[None, :] < n_slc))
    ov = jnp.asarray(ov.astype(np.float32), dtype=BF16)
    nkeys = (n_pages + 1) * lanes
    ex = jnp.asarray((np.arange(nkeys)[None, :] // BLK_SLC == np.arange(lanes)[:, None]).astype(np.float32), dtype=BF16)

    full = lambda shape: pl.BlockSpec(shape, lambda b, pt: (0,) * len(shape))
    per_b = lambda shape: pl.BlockSpec((1,) + shape, lambda b, pt: (b,) + (0,) * len(shape))
    page = lambda p, kind: pl.BlockSpec((1, 1, N_KV, 2, HEAD_DIM, PAGE_SIZE),
                                        lambda b, pt: (layer, pt[b, p], 0, kind, 0, 0))
    cmp_page = lambda p: page(p, 0)
    slc_page = lambda p: page(p, 1)
    in_specs = ([per_b((N_KV, rows, lanes)), per_b((N_KV, rows, lanes)), per_b((N_KV, 8, lanes)),
                 per_b((N_KV, 8, lanes)),
                 pl.BlockSpec((1, 1, N_KV, 2, HEAD_DIM, wlen), lambda b, pt: (layer, b, 0, 0, 0, 0))]
                + [cmp_page(p) for p in range(n_pages)]
                + [slc_page(p) for p in range(n_pages)]
                + [full((2, 1, kd)), full((2, 1, kd)), full((2, kd, CMP_HID)), full((2, kd, CMP_HID)),
                   full((2, D_CMP, HEAD_DIM, 2 * CMP_HID)), full((PAGE_SIZE, PAGE_SIZE)),
                   full((2, 1, CMP_HID)), full((2, CMP_HID, HEAD_DIM)), full((lanes, lanes)), full((lanes, nkeys))])
    o = pl.pallas_call(
        functools.partial(_nsa_sample_kernel, n_pages=n_pages, past=past, t_new=t_new),
        grid_spec=pltpu.PrefetchScalarGridSpec(
            num_scalar_prefetch=1, grid=(bsz,), in_specs=in_specs,
            out_specs=pl.BlockSpec((1, N_KV, rows, HEAD_DIM), lambda b, pt: (b, 0, 0, 0))),
        out_shape=jax.ShapeDtypeStruct((bsz, N_KV, rows, HEAD_DIM), F32),
        compiler_params=_params(("arbitrary",)),
        name="nsa_sample",
    )(page_table, q_pad, zng, new_rows(2), new_rows(4), win_t,
      *([pool_t] * (2 * n_pages)),
      pa, pb, wa, wb, wab, sel, b1.reshape(2, 1, CMP_HID), w2.astype(BF16), ov, ex)
    o = o.reshape(bsz, N_KV, GRP, t_new, HEAD_DIM).transpose(0, 3, 1, 2, 4).reshape(bsz * t_new, N_HEADS * HEAD_DIM)
    return o, kv6


HG_SUB = 16
HG_BLOCK = 256


def _lower_bound(lbl, layer):
    e = jnp.exp(lbl - jnp.max(lbl, axis=0, keepdims=True))
    p = e / jnp.sum(e, axis=0, keepdims=True)
    lb = jnp.zeros((1, lbl.shape[1]), F32)
    for i in range(1, layer + 1):
        lb = lb + p[i:i + 1, :]
    return lb


def _hgrn_kernel(zq_ref, zf_ref, zi_ref, zg_ref, lbl_ref, ng_ref, ltri_ref, o_ref, s_ref, st_s, *, layer):
    i = pl.program_id(1)
    tb = zq_ref.shape[0]
    c = HG_SUB
    nc = tb // c

    @pl.when(i == 0)
    def _():
        st_s[...] = jnp.zeros(st_s.shape, F32)

    lb = _lower_bound(lbl_ref[...], layer)
    q = _silu(zq_ref[...])
    f = lb + (1.0 - lb) * _sigmoid(zf_ref[...])
    k = 1.0 - f
    v = zi_ref[...]
    logf = jnp.log(f)
    h1 = logf.astype(BF16)
    r1 = logf - h1.astype(F32)
    h2 = r1.astype(BF16)
    h3 = (r1 - h2.astype(F32)).astype(BF16)
    ltri = ltri_ref[...]
    g = (jnp.dot(ltri, h1, preferred_element_type=F32) + jnp.dot(ltri, h2, preferred_element_type=F32)
         + jnp.dot(ltri, h3, preferred_element_type=F32))
    g3 = g.reshape(nc, c, DK_HG)
    q3 = q.reshape(nc, c, DK_HG)
    k3 = k.reshape(nc, c, DK_HG)
    v3 = v.reshape(nc, c, DV_HG)
    g_last = g3[:, c - 1:c, :]
    qe = (q * jnp.exp(g)).astype(BF16)
    kd = (k3 * jnp.exp(g_last - g3)).reshape(tb, DK_HG).astype(BF16)
    dec = jnp.exp(g_last)

    tpos = lax.broadcasted_iota(jnp.int32, (nc, c, DK_HG), 1)
    o_intra = jnp.zeros((nc, c, DV_HG), F32)
    for s in range(c):
        d = jnp.exp(jnp.where(tpos >= s, g3 - g3[:, s:s + 1, :], 0.0))
        w = jnp.where(tpos >= s, q3 * k3[:, s:s + 1, :] * d, 0.0)
        o_intra = o_intra + jnp.sum(w, axis=-1, keepdims=True) * v3[:, s:s + 1, :]

    vT = v.T
    lane_chunk = lax.broadcasted_iota(jnp.int32, (DV_HG, tb), 1) // c
    st = st_s[...]
    o_inter = []
    for ci in range(nc):
        o_inter.append(lax.dot_general(qe[ci * c:(ci + 1) * c, :], st.astype(BF16), (((1,), (1,)), ((), ())),
                                       preferred_element_type=F32))
        ut = jnp.dot(jnp.where(lane_chunk == ci, vT, 0.0).astype(BF16), kd, preferred_element_type=F32)
        st = st * dec[ci] + ut
    st_s[...] = st
    o = jnp.concatenate(o_inter, axis=0) + o_intra.reshape(tb, DV_HG)
    o = o * lax.rsqrt(jnp.mean(o * o, axis=-1, keepdims=True) + LN_EPS) * ng_ref[...]
    o_ref[...] = (o * _silu(zg_ref[...])).astype(o_ref.dtype)

    @pl.when(i == pl.num_programs(1) - 1)
    def _():
        s_ref[0] = st.T


def hgrn_prompt(z, lb_logits, norm_g, layer):
    t = z.shape[0]
    tb = min(HG_BLOCK, t)
    r = np.arange(tb)
    ltri = jnp.asarray(((r[:, None] // HG_SUB == r[None, :] // HG_SUB) & (r[None, :] <= r[:, None])).astype(np.float32),
                       dtype=BF16)
    col = lambda cidx: pl.BlockSpec((tb, DK_HG), lambda h, i, o=(OFF_HG + cidx * H_HG * DK_HG) // DK_HG: (i, o + h))
    return pl.pallas_call(
        functools.partial(_hgrn_kernel, layer=layer),
        grid=(H_HG, t // tb),
        in_specs=[col(0), col(1), col(2), col(3),
                  pl.BlockSpec((DEPTH, DK_HG), lambda h, i: (0, h)),
                  pl.BlockSpec((1, DV_HG), lambda h, i: (0, h)),
                  pl.BlockSpec((tb, tb), lambda h, i: (0, 0))],
        out_specs=[pl.BlockSpec((tb, DV_HG), lambda h, i: (i, h)),
                   pl.BlockSpec((1, DK_HG, DV_HG), lambda h, i: (h, 0, 0))],
        out_shape=[jax.ShapeDtypeStruct((t, H_HG * DV_HG), BF16),
                   jax.ShapeDtypeStruct((H_HG, DK_HG, DV_HG), F32)],
        scratch_shapes=[pltpu.VMEM((DV_HG, DK_HG), F32)],
        compiler_params=_params(("parallel", "arbitrary")),
        name="hgrn_prompt",
    )(z, z, z, z, lb_logits, norm_g, ltri)


def _hgrn_step_kernel(zq_ref, zf_ref, zi_ref, zg_ref, s0_ref, lbl_ref, ng_ref, o_ref, s_ref, *, layer):
    t_new = zq_ref.shape[1]
    lb = _lower_bound(lbl_ref[...], layer)
    q_all = _silu(zq_ref[0])
    f_all = lb + (1.0 - lb) * _sigmoid(zf_ref[0])
    v_all = zi_ref[0]
    zg_all = zg_ref[0]
    ng = ng_ref[...]
    tpos = lax.broadcasted_iota(jnp.int32, (t_new, DK_HG), 0)
    pad_rows = jnp.zeros((8 - t_new, DK_HG), F32)
    outs = []
    for h in range(H_HG):
        sl = slice(h * DK_HG, (h + 1) * DK_HG)
        q, f, v = q_all[:, sl], f_all[:, sl], v_all[:, sl]
        k = 1.0 - f
        logf = jnp.log(f)
        rows = [logf[0:1, :]]
        for t in range(1, t_new):
            rows.append(rows[-1] + logf[t:t + 1, :])
        g = jnp.concatenate(rows, axis=0)
        g_last = rows[-1]
        s0 = s0_ref[0, h]
        qe = jnp.concatenate([q * jnp.exp(g), pad_rows], axis=0)
        o = _bdot(qe, s0)[0:t_new]
        for s in range(t_new):
            d = jnp.exp(jnp.where(tpos >= s, g - g[s:s + 1, :], 0.0))
            w = jnp.where(tpos >= s, q * k[s:s + 1, :] * d, 0.0)
            o = o + jnp.sum(w, axis=-1, keepdims=True) * v[s:s + 1, :]
        kd = k * jnp.exp(g_last - g)
        cols = jnp.concatenate([jnp.exp(g_last), kd, jnp.zeros((DK_HG - 1 - t_new, DK_HG), F32)], axis=0).T
        s_new = s0 * cols[:, 0:1]
        for s in range(t_new):
            s_new = s_new + cols[:, 1 + s:2 + s] * v[s:s + 1, :]
        s_ref[0, h] = s_new
        o = o * lax.rsqrt(jnp.mean(o * o, axis=-1, keepdims=True) + LN_EPS) * ng[:, sl]
        outs.append(o * _silu(zg_all[:, sl]))
    o_ref[0] = jnp.concatenate(outs, axis=1).astype(o_ref.dtype)


def hgrn_step(z3, s0, lb_logits, norm_g, layer):
    bsz, t_new = z3.shape[:2]
    w = H_HG * DK_HG
    col = lambda cidx: pl.BlockSpec((1, t_new, w), lambda b, o=(OFF_HG + cidx * w) // w: (b, 0, o))
    assert OFF_HG % w == 0
    st = pl.BlockSpec((1, H_HG, DK_HG, DV_HG), lambda b: (b, 0, 0, 0))
    return pl.pallas_call(
        functools.partial(_hgrn_step_kernel, layer=layer),
        grid=(bsz,),
        in_specs=[col(0), col(1), col(2), col(3), st,
                  pl.BlockSpec((DEPTH, w), lambda b: (0, 0)), pl.BlockSpec((1, w), lambda b: (0, 0))],
        out_specs=[pl.BlockSpec((1, t_new, w), lambda b: (b, 0, 0)), st],
        out_shape=[jax.ShapeDtypeStruct((bsz, t_new, w), BF16), jax.ShapeDtypeStruct(s0.shape, F32)],
        compiler_params=_params(("parallel",)),
        name="hgrn_step",
    )(z3, z3, z3, z3, s0, lb_logits, norm_g)


CONV_HALO = 32


def _glu_ln_silu(y, g_ref, b_ref):
    mu = jnp.mean(y, axis=-1, keepdims=True)
    d = y - mu
    var = jnp.mean(d * d, axis=-1, keepdims=True)
    return _silu(d * lax.rsqrt(var + LN_EPS) * g_ref[...] + b_ref[...])


def _conv_kernel(a_ref, g_ref, w_ref, b_ref, lg_ref, lb_ref, o_ref, tail_ref, buf_s):
    i = pl.program_id(0)
    tb = a_ref.shape[0]

    @pl.when(i == 0)
    def _():
        buf_s[0:CONV_HALO, :] = jnp.zeros((CONV_HALO, D_CONV), F32)

    buf_s[CONV_HALO:, :] = a_ref[...] * _sigmoid(g_ref[...])
    off = CONV_HALO - (CONV_W - 1)
    y = jnp.zeros((tb, D_CONV), F32) + b_ref[...]
    for k in range(CONV_W):
        y = y + buf_s[off + k:off + k + tb, :] * w_ref[k:k + 1, :]
    o_ref[...] = _glu_ln_silu(y, lg_ref, lb_ref).astype(o_ref.dtype)
    tail = buf_s[tb:tb + CONV_HALO, :]
    tail_ref[...] = tail
    buf_s[0:CONV_HALO, :] = tail


def conv_prompt(z, w_dw, b_dw, ln_g, ln_b, tb=512):
    t = z.shape[0]
    tb = min(tb, t)
    vec = pl.BlockSpec((1, D_CONV), lambda i: (0, 0))
    o, tail = pl.pallas_call(
        _conv_kernel,
        grid=(t // tb,),
        in_specs=[pl.BlockSpec((tb, D_CONV), lambda i: (i, OFF_CONV // D_CONV)),
                  pl.BlockSpec((tb, D_CONV), lambda i: (i, OFF_CONV // D_CONV + 1)),
                  pl.BlockSpec((CONV_W, D_CONV), lambda i: (0, 0)), vec, vec, vec],
        out_specs=[pl.BlockSpec((tb, D_CONV), lambda i: (i, 0)), pl.BlockSpec((CONV_HALO, D_CONV), lambda i: (0, 0))],
        out_shape=[jax.ShapeDtypeStruct((t, D_CONV), BF16), jax.ShapeDtypeStruct((CONV_HALO, D_CONV), F32)],
        scratch_shapes=[pltpu.VMEM((tb + CONV_HALO, D_CONV), F32)],
        compiler_params=_params(("arbitrary",)),
        name="conv_prompt",
    )(z, z, w_dw, b_dw[None], ln_g[None], ln_b[None])
    return o, tail[CONV_HALO - (CONV_W - 1):]


def _conv_step_kernel(a_ref, g_ref, st_ref, wsh_ref, wnew_ref, b_ref, lg_ref, lb_ref, o_ref, ns_ref):
    t_new = a_ref.shape[1]
    hist = CONV_W - 1
    u = [a_ref[:, s, :] * _sigmoid(g_ref[:, s, :]) for s in range(t_new)]
    st = st_ref[...]
    for t in range(t_new):
        y = jnp.sum(st * wsh_ref[t], axis=1) + b_ref[...]
        for s in range(t + 1):
            y = y + u[s] * wnew_ref[t, s:s + 1, :]
        o_ref[:, t, :] = _glu_ln_silu(y, lg_ref, lb_ref).astype(o_ref.dtype)
    ns_ref[:, 0:hist - t_new, :] = st_ref[:, t_new:, :]
    for s in range(t_new):
        ns_ref[:, hist - t_new + s, :] = u[s]


def conv_step(z3, state, w_dw, b_dw, ln_g, ln_b, bb=8):
    bsz, t_new = z3.shape[:2]
    hist = CONV_W - 1
    wsh = jnp.stack([jnp.concatenate([jnp.zeros((t, D_CONV), F32), w_dw[:hist - t]], axis=0) for t in range(t_new)])
    wnew = jnp.stack([jnp.concatenate([w_dw[hist - t:], jnp.zeros((t_new - 1 - t, D_CONV), F32)], axis=0)
                      for t in range(t_new)])
    vec = pl.BlockSpec((1, D_CONV), lambda i: (0, 0))
    return pl.pallas_call(
        _conv_step_kernel,
        grid=(bsz // bb,),
        in_specs=[pl.BlockSpec((bb, t_new, D_CONV), lambda i: (i, 0, OFF_CONV // D_CONV)),
                  pl.BlockSpec((bb, t_new, D_CONV), lambda i: (i, 0, OFF_CONV // D_CONV + 1)),
                  pl.BlockSpec((bb, hist, D_CONV), lambda i: (i, 0, 0)),
                  pl.BlockSpec((t_new, hist, D_CONV), lambda i: (0, 0, 0)),
                  pl.BlockSpec((t_new, t_new, D_CONV), lambda i: (0, 0, 0)), vec, vec, vec],
        out_specs=[pl.BlockSpec((bb, t_new, D_CONV), lambda i: (i, 0, 0)),
                   pl.BlockSpec((bb, hist, D_CONV), lambda i: (i, 0, 0))],
        out_shape=[jax.ShapeDtypeStruct((bsz, t_new, D_CONV), BF16), jax.ShapeDtypeStruct(state.shape, F32)],
        compiler_params=_params(("parallel",)),
        name="conv_step",
    )(z3, z3, state, wsh, wnew, b_dw[None], ln_g[None], ln_b[None])


def _relayout_w_in(w):
    o_conv, o_q, o_kv, o_ng = 0, 2048, 3072, 4608
    o_h, o_mg, end = 4656, 8752, 14896
    parts = [w[:, o_mg:end], w[:, o_conv:o_q], w[:, o_q:o_kv], w[:, o_h:o_mg], w[:, o_kv:o_ng], w[:, o_ng:o_h],
             jnp.zeros((w.shape[0], NZ - OFF_NG - 3 * N_HEADS), w.dtype)]
    return jnp.concatenate(parts, axis=1).astype(BF16)


def _route_weights(wg, bg, we, be):
    pad = ROUTE_W - N_EXPERTS - N_GROUPS
    w = jnp.concatenate([we, wg, jnp.zeros((we.shape[0], pad), F32)], axis=1)
    b = jnp.concatenate([be, bg, jnp.zeros((pad,), F32)])[None, :]
    return w, b


def _layer(x, mods, P, l, conv_prev, s0, batch, nsa_state):
    m = x.shape[0]
    t = m // batch
    (sh1, sc1, g1), (sh2, sc2, g2) = mods
    u = modulate(x, sc1, sh1)
    z = matmul(u, P['w_in_r'][l], tn=NZ_TILE)
    z3 = z.reshape(batch, t, NZ)
    conv_w = (P['w_dw'][l], P['b_dw'][l], P['conv_ln_g'][l], P['conv_ln_b'][l])
    cmp_w = (P['w_cmp_pe'][l], P['w_cmp_1'][l], P['b_cmp_1'][l], P['w_cmp_2'][l])
    if nsa_state is None:
        a_conv, conv_new = conv_prompt(z, *conv_w)
        conv_new = conv_new[None]
        o_hg, s_new = hgrn_prompt(z, P['hgrn_lb_logits'], P['hgrn_norm_g'][l][None], l)
        s_new = s_new[None]
        o_nsa, kv6 = nsa_prompt(z, *cmp_w)
        kv6 = kv6[None]
        win_new = kv6[:, -min(WINDOW, t):, :, 4:6]
    else:
        a_conv, conv_new = conv_step(z3, conv_prev, *conv_w)
        o_hg, s_new = hgrn_step(z3, s0, P['hgrn_lb_logits'], P['hgrn_norm_g'][l][None], l)
        pool_t, win_t, page_table, win_buf = nsa_state
        o_nsa, kv6 = nsa_sample(z, pool_t, win_t, l, page_table, *cmp_w)
        win_new = jnp.concatenate([win_buf, kv6[..., 4:6, :]], axis=1)[:, -win_buf.shape[1]:]
    merged = merge_branches(a_conv.reshape(m, D_CONV), o_nsa, o_hg.reshape(m, -1),
                            P['w_conv_out'][l], P['w_proj_nsa'][l], P['w_proj_hgrn'][l], z)
    y = matmul(merged, P['w_out'][l], tn=512)
    w_r, b_r = _route_weights(P['w_route_group'][l], P['b_route_group'][l], P['w_route_expert'][l], P['b_route_expert'][l])
    x1, u2, comb = ln_residual_route(x, y, g1, P['ln_g'][l, 0][None], P['ln_b'][l, 0][None], sc2, sh2, w_r, b_r)
    y2 = moe_experts(u2, comb, P['w_exp_gate'][l], P['w_exp_up'][l], P['w_exp_down'][l])
    x2 = ln_residual(x1, y2, g2, P['ln_g'][l, 1][None], P['ln_b'][l, 1][None])
    return x2, (kv6[..., 0:4, :], win_new, conv_new, s_new)


def kernel(x_prompt, x_sample, cache_nsa_kv, state_win_kv, state_conv, state_hgrn, page_table,
           c_prompt, c_sample, w_ada, b_ada, w_in, w_cmp_pe, w_cmp_1, b_cmp_1, w_cmp_2,
           w_dw, b_dw, conv_ln_g, conv_ln_b, w_conv_out, hgrn_lb_logits, hgrn_norm_g,
           w_proj_nsa, w_proj_hgrn, w_out, ln_g, ln_b, w_route_group, b_route_group,
           w_route_expert, b_route_expert, w_exp_gate, w_exp_up, w_exp_down):
    P = dict(w_cmp_pe=w_cmp_pe, w_cmp_1=w_cmp_1, b_cmp_1=b_cmp_1, w_cmp_2=w_cmp_2, w_dw=w_dw, b_dw=b_dw,
             conv_ln_g=conv_ln_g, conv_ln_b=conv_ln_b, w_conv_out=w_conv_out, hgrn_norm_g=hgrn_norm_g,
             w_proj_nsa=w_proj_nsa, w_proj_hgrn=w_proj_hgrn, w_out=w_out, ln_g=ln_g, ln_b=ln_b,
             w_route_group=w_route_group, b_route_group=b_route_group, w_route_expert=w_route_expert,
             b_route_expert=b_route_expert, w_exp_gate=w_exp_gate, w_exp_up=w_exp_up, w_exp_down=w_exp_down)
    P['w_in_r'] = [_relayout_w_in(w_in[l]) for l in range(DEPTH)]
    P['hgrn_lb_logits'] = hgrn_lb_logits

    bp, tp = x_prompt.shape[:2]
    bs, ts = x_sample.shape[:2]
    c_all = jnp.concatenate([c_sample, c_prompt, jnp.zeros((8 - bp % 8, D_MODEL), F32)], axis=0)
    ada = ada_all(c_all, w_ada, b_ada)

    def mods(l, i, rows, rep):
        mrow = ada[2 * l + i, rows]
        if rep > 1:
            mrow = jnp.repeat(mrow, rep, axis=0)
        return mrow[:, :D_MODEL], mrow[:, D_MODEL:2 * D_MODEL], mrow[:, 2 * D_MODEL:]

    pool_t = cache_nsa_kv.transpose(0, 1, 3, 4, 5, 2)
    win_t = state_win_kv.transpose(0, 1, 3, 4, 5, 2)

    xp = x_prompt.reshape(bp * tp, D_MODEL)
    xs = x_sample.reshape(bs * ts, D_MODEL)
    outs_p, outs_s = [], []
    for l in range(DEPTH):
        mp = [mods(l, i, slice(bs, bs + bp), 1) for i in range(2)]
        xp, st = _layer(xp, mp, P, l, None, None, bp, None)
        outs_p.append(st)
        ms = [mods(l, i, slice(0, bs), ts) for i in range(2)]
        xs, st = _layer(xs, ms, P, l, state_conv[l], state_hgrn[l], bs,
                        (pool_t, win_t, page_table, state_win_kv[l]))
        outs_s.append(st)
    stack = lambda outs, i: jnp.stack([o[i] for o in outs])
    return (xp.reshape(bp, tp, D_MODEL), xs.reshape(bs, ts, D_MODEL),
            stack(outs_p, 0), stack(outs_p, 1), stack(outs_p, 2), stack(outs_p, 3),
            stack(outs_s, 0), stack(outs_s, 1), stack(outs_s, 2), stack(outs_s, 3))
```
